```python
import math, functools
import jax, jax.numpy as jnp
from jax import lax
import numpy as np

D_MODEL = 2048
BATCH = 4
SEQ = 2048
DEPTH = 2
DEC_BATCH = 128
DEC_SEQ = 1
PAST_LEN = 16384
PAGE_SIZE = 128

N_META = 16
MIX_WIDTH = 2 * D_MODEL
CHUNK = 128
SSD_WIDTH = MIX_WIDTH // 2
SSD_HEAD_DIM = 64
SSD_HEADS = SSD_WIDTH // SSD_HEAD_DIM
SSD_GROUPS = 4
SSD_HPG = SSD_HEADS // SSD_GROUPS
SSD_STATE = 128
CONV_W = 4
CONV_CH = SSD_WIDTH + 2 * SSD_GROUPS * SSD_STATE
MLSTM_WIDTH = MIX_WIDTH - SSD_WIDTH
MLSTM_HEADS = 8
MLSTM_DV = MLSTM_WIDTH // MLSTM_HEADS
MLSTM_DK = MLSTM_DV // 2
PEER_HEADS = 8
PEER_KEYS = 128
PEER_TOPK = 16
PEER_DK = 256
PEER_HALF = PEER_DK // 2
N_EXPERTS = PEER_KEYS * PEER_KEYS
PEER_BLOCK = 128
COL_SIZES = (SSD_WIDTH, CONV_CH, SSD_HEADS, MLSTM_HEADS * MLSTM_DK, MLSTM_HEADS * MLSTM_DK, MLSTM_WIDTH, MLSTM_HEADS, MLSTM_HEADS, MLSTM_WIDTH)
IN_COLS = sum(COL_SIZES)

kernel_name = "hymba_ssd_mlstm_peer_step"


def rmsnorm(x, g, eps=1e-6):
    xf = x.astype(jnp.float32)
    y = xf * lax.rsqrt(jnp.mean(xf * xf, axis=-1, keepdims=True) + eps)
    return (y * g.astype(jnp.float32)).astype(x.dtype)


def split_cols(a, sizes):
    out, o = [], 0
    for s in sizes:
        out.append(a[..., o:o + s])
        o += s
    return out


def causal_conv(xbc, hist, w, b):
    L = xbc.shape[1]
    full = jnp.concatenate([hist.astype(xbc.dtype), xbc], axis=1)
    out = b + sum(w[k] * full[:, k:k + L] for k in range(CONV_W))
    return jax.nn.silu(out), full[:, -(CONV_W - 1):]


def to_chunks(a, chunk):
    b, L = a.shape[:2]
    a = a.reshape((b, L // chunk, chunk) + a.shape[2:])
    return jnp.moveaxis(a, 1, 0)


def from_chunks(a):
    n, b, c = a.shape[:3]
    return jnp.moveaxis(a, 0, 1).reshape((b, n * c) + a.shape[3:])


def run_chunked(step, state, seqs, segments):
    outs = []
    for start, length, chunk in segments:
        xs = tuple(to_chunks(s[:, start:start + length], chunk) for s in seqs)
        state, ys = lax.scan(step, state, xs)
        outs.append(from_chunks(ys))
    y = outs[0] if len(outs) == 1 else jnp.concatenate(outs, axis=1)
    return state, y


def ssd_chunk(state, inp, A):
    x, dt, Bm, Cm = [t.astype(jnp.float32) for t in inp]
    l = x.shape[1]
    cum = jnp.cumsum(dt * A, axis=1)
    mask = jnp.tril(jnp.ones((l, l), dtype=bool))[None, :, :, None, None]
    seg = cum[:, :, None] - cum[:, None, :]
    Lmat = jnp.exp(jnp.where(mask, seg, -jnp.inf))
    CB = jnp.einsum('btgn,bsgn->btsg', Cm, Bm)
    xdt = x * dt[..., None]
    y = jnp.einsum('btsgh,bsghp->btghp', CB[..., None] * Lmat, xdt)
    y = y + jnp.einsum('btgn,bghpn->btghp', Cm, state) * jnp.exp(cum)[..., None]
    decay_end = jnp.exp(cum[:, -1:] - cum)
    new_state = state * jnp.exp(cum[:, -1])[..., None, None] + jnp.einsum('bsgn,bsgh,bsghp->bghpn', Bm, decay_end, xdt)
    return new_state, y


def mlstm_chunk(carry, inp):
    Cs, ns, m = carry
    q, k, v, li, lf = [t.astype(jnp.float32) for t in inp]
    l = q.shape[1]
    bc = jnp.cumsum(lf, axis=1)
    mask = jnp.tril(jnp.ones((l, l), dtype=bool))[None, :, :, None]
    Dm = jnp.where(mask, bc[:, :, None] - bc[:, None, :] + li[:, None, :], -jnp.inf)
    inter = bc + m[:, None]
    m_t = jnp.maximum(inter, jnp.max(Dm, axis=2))
    w_intra = jnp.exp(Dm - m_t[:, :, None])
    w_inter = jnp.exp(inter - m_t)
    qkw = jnp.einsum('bthd,bshd->btsh', q, k) * w_intra
    num = jnp.einsum('btsh,bshv->bthv', qkw, v) + w_inter[..., None] * jnp.einsum('bthd,bhdv->bthv', q, Cs)
    den = jnp.sum(qkw, axis=2) + w_inter * jnp.einsum('bthd,bhd->bth', q, ns)
    h = num / jnp.maximum(jnp.abs(den), jnp.exp(-m_t))[..., None]
    end_inter = bc[:, -1] + m
    end_intra = bc[:, -1:] - bc + li
    m_new = jnp.maximum(end_inter, jnp.max(end_intra, axis=1))
    we = jnp.exp(end_intra - m_new[:, None])
    carry_scale = jnp.exp(end_inter - m_new)
    C_new = carry_scale[..., None, None] * Cs + jnp.einsum('bsh,bshd,bshv->bhdv', we, k, v)
    n_new = carry_scale[..., None] * ns + jnp.einsum('bsh,bshd->bhd', we, k)
    return (C_new, n_new, m_new), h


def mixer(hn, st_ssd, st_conv, st_c, st_n, st_m, segments, w_in, conv_w, conv_b, dt_bias, a_log, d_skip, g_ssd_out, i_bias, f_bias, g_mlstm_out, w_out):
    b, L, _ = hn.shape
    proj = hn @ w_in
    z, xbc, dt_raw, q, k, v, ig, fg, og = split_cols(proj, COL_SIZES)
    xbc, new_conv = causal_conv(xbc, st_conv, conv_w, conv_b)
    xs, Bm, Cm = split_cols(xbc, (SSD_WIDTH, SSD_GROUPS * SSD_STATE, SSD_GROUPS * SSD_STATE))
    xs = xs.reshape(b, L, SSD_GROUPS, SSD_HPG, SSD_HEAD_DIM)
    Bm = Bm.reshape(b, L, SSD_GROUPS, SSD_STATE)
    Cm = Cm.reshape(b, L, SSD_GROUPS, SSD_STATE)
    dt = jax.nn.softplus(dt_raw.astype(jnp.float32) + dt_bias.astype(jnp.float32)).reshape(b, L, SSD_GROUPS, SSD_HPG)
    A = -jnp.exp(a_log.astype(jnp.float32)).reshape(SSD_GROUPS, SSD_HPG)
    s0 = st_ssd.astype(jnp.float32).reshape(b, SSD_GROUPS, SSD_HPG, SSD_HEAD_DIM, SSD_STATE)
    new_ssd, y = run_chunked(lambda c, i: ssd_chunk(c, i, A), s0, (xs, dt, Bm, Cm), segments)
    y = y + d_skip.astype(jnp.float32).reshape(SSD_GROUPS, SSD_HPG)[..., None] * xs.astype(jnp.float32)
    y = y.reshape(b, L, SSD_WIDTH)
    y_ssd = rmsnorm(y * jax.nn.silu(z.astype(jnp.float32)), g_ssd_out)
    q = q.reshape(b, L, MLSTM_HEADS, MLSTM_DK)
    k = k.reshape(b, L, MLSTM_HEADS, MLSTM_DK) * (MLSTM_DK ** -0.5)
    v = v.reshape(b, L, MLSTM_HEADS, MLSTM_DV)
    li = ig.astype(jnp.float32) + i_bias.astype(jnp.float32)
    lf = jax.nn.log_sigmoid(fg.astype(jnp.float32) + f_bias.astype(jnp.float32))
    m0 = (st_c.astype(jnp.float32), st_n.astype(jnp.float32), st_m.astype(jnp.float32))
    (c1, n1, mm1), hm = run_chunked(mlstm_chunk, m0, (q, k, v, li, lf), segments)
    hm = rmsnorm(hm, g_mlstm_out.reshape(MLSTM_HEADS, MLSTM_DV))
    hm = (hm * jax.nn.sigmoid(og.astype(jnp.float32)).reshape(b, L, MLSTM_HEADS, MLSTM_DV)).reshape(b, L, MLSTM_WIDTH)
    out = jnp.concatenate([y_ssd, hm], axis=-1).astype(hn.dtype) @ w_out
    new_ssd = new_ssd.reshape(b, SSD_HEADS, SSD_HEAD_DIM, SSD_STATE)
    return out, (new_ssd, new_conv, c1, n1, mm1)


def peer(x, w_q, sub_keys, u_tab, v_tab):
    T = x.shape[0]
    q = (x @ w_q).astype(jnp.float32).reshape(T, PEER_HEADS, 2, PEER_HALF)
    s = jnp.einsum('thcd,chkd->thck', q, sub_keys.astype(jnp.float32))
    v1, i1 = lax.top_k(s[:, :, 0], PEER_TOPK)
    v2, i2 = lax.top_k(s[:, :, 1], PEER_TOPK)
    cand = (v1[..., :, None] + v2[..., None, :]).reshape(T, PEER_HEADS, PEER_TOPK * PEER_TOPK)
    sc, ci = lax.top_k(cand, PEER_TOPK)
    e_idx = jnp.take_along_axis(i1, ci // PEER_TOPK, axis=-1) * PEER_KEYS + jnp.take_along_axis(i2, ci % PEER_TOPK, axis=-1)
    g = jax.nn.softmax(sc, axis=-1)
    n_blk = -(-T // PEER_BLOCK)
    pad = n_blk * PEER_BLOCK - T
    xp = jnp.pad(x, ((0, pad), (0, 0))).reshape(n_blk, PEER_BLOCK, D_MODEL)
    ep = jnp.pad(e_idx, ((0, pad), (0, 0), (0, 0))).reshape(n_blk, PEER_BLOCK, PEER_HEADS, PEER_TOPK)
    gp = jnp.pad(g, ((0, pad), (0, 0), (0, 0))).reshape(n_blk, PEER_BLOCK, PEER_HEADS, PEER_TOPK)

    def blk(args):
        xb, eb, gb = args
        a = jnp.einsum('td,thkd->thk', xb, u_tab[eb]).astype(jnp.float32)
        a = jax.nn.gelu(a, approximate=False) * gb
        return jnp.einsum('thk,thkd->td', a.astype(v_tab.dtype), v_tab[eb])

    out = lax.map(blk, (xp, ep, gp)).reshape(n_blk * PEER_BLOCK, D_MODEL)[:T]
    return out.astype(x.dtype)


def run_trunk(h, st_ssd, st_conv, st_c, st_n, st_m, segments, g_mix, w_in, conv_w, conv_b, dt_bias, a_log, d_skip, g_ssd_out, mlstm_i_bias, mlstm_f_bias, g_mlstm_out, w_out, g_ffn, peer_w_q, peer_sub_keys, peer_u, peer_v, g_final):
    new = ([], [], [], [], [])
    for l in range(DEPTH):
        hn = rmsnorm(h, g_mix[l])
        mix, states = mixer(hn, st_ssd[l], st_conv[l], st_c[l], st_n[l], st_m[l], segments, w_in[l], conv_w[l], conv_b[l], dt_bias[l], a_log[l], d_skip[l], g_ssd_out[l], mlstm_i_bias[l], mlstm_f_bias[l], g_mlstm_out[l], w_out[l])
        h = h + mix.astype(h.dtype)
        b, L, _ = h.shape
        hn2 = rmsnorm(h, g_ffn[l]).reshape(b * L, D_MODEL)
        h = h + peer(hn2, peer_w_q[l], peer_sub_keys[l], peer_u[l], peer_v[l]).reshape(h.shape).astype(h.dtype)
        for lst, s in zip(new, states):
            lst.append(s)
    return rmsnorm(h, g_final), tuple(jnp.stack(s) for s in new)


def setup_inputs(seed: int = 0) -> dict:
    key = jax.random.key(seed)
    ks = jax.random.split(key, 32)
    nrm = lambda k, shape, s=1.0: s * jax.random.normal(k, shape, jnp.float32)
    dt = jnp.exp(jax.random.uniform(ks[14], (DEPTH, SSD_HEADS), jnp.float32, math.log(1e-3), math.log(1e-1)))
    return {
        "x_prompt": nrm(ks[0], (BATCH, SEQ, D_MODEL)),
        "x_sample": nrm(ks[1], (DEC_BATCH, DEC_SEQ, D_MODEL)),
        "state_ssd": nrm(ks[2], (DEPTH, DEC_BATCH, SSD_HEADS, SSD_HEAD_DIM, SSD_STATE), 0.5),
        "state_conv": nrm(ks[3], (DEPTH, DEC_BATCH, CONV_W - 1, CONV_CH)),
        "state_mlstm_c": nrm(ks[4], (DEPTH, DEC_BATCH, MLSTM_HEADS, MLSTM_DK, MLSTM_DV)),
        "state_mlstm_n": nrm(ks[5], (DEPTH, DEC_BATCH, MLSTM_HEADS, MLSTM_DK)),
        "state_mlstm_m": nrm(ks[6], (DEPTH, DEC_BATCH, MLSTM_HEADS)),
        "meta_tokens": nrm(ks[7], (N_META, D_MODEL)),
        "g_mix": 1.0 + nrm(ks[8], (DEPTH, D_MODEL), 0.02),
        "w_in": nrm(ks[9], (DEPTH, D_MODEL, IN_COLS), D_MODEL ** -0.5),
        "conv_w": nrm(ks[10], (DEPTH, CONV_W, CONV_CH), CONV_W ** -0.5),
        "conv_b": nrm(ks[11], (DEPTH, CONV_CH), 0.02),
        "dt_bias": dt + jnp.log(-jnp.expm1(-dt)),
        "a_log": jnp.log(jax.random.uniform(ks[12], (DEPTH, SSD_HEADS), jnp.float32, 1.0, 16.0)),
        "d_skip": 1.0 + nrm(ks[13], (DEPTH, SSD_HEADS), 0.1),
        "g_ssd_out": 1.0 + nrm(ks[15], (DEPTH, SSD_WIDTH), 0.02),
        "mlstm_i_bias": nrm(ks[16], (DEPTH, MLSTM_HEADS), 0.1),
        "mlstm_f_bias": jnp.linspace(3.0, 6.0, MLSTM_HEADS, dtype=jnp.float32)[None] + nrm(ks[17], (DEPTH, MLSTM_HEADS), 0.1),
        "g_mlstm_out": 1.0 + nrm(ks[18], (DEPTH, MLSTM_WIDTH), 0.02),
        "w_out": nrm(ks[19], (DEPTH, MIX_WIDTH, D_MODEL), MIX_WIDTH ** -0.5),
        "g_ffn": 1.0 + nrm(ks[20], (DEPTH, D_MODEL), 0.02),
        "peer_w_q": nrm(ks[21], (DEPTH, D_MODEL, PEER_HEADS * PEER_DK), D_MODEL ** -0.5),
        "peer_sub_keys": nrm(ks[22], (DEPTH, 2, PEER_HEADS, PEER_KEYS, PEER_HALF), PEER_HALF ** -0.5),
        "peer_u": nrm(ks[23], (DEPTH, N_EXPERTS, D_MODEL), D_MODEL ** -0.5),
        "peer_v": nrm(ks[24], (DEPTH, N_EXPERTS, D_MODEL), PEER_HEADS ** -0.5),
        "g_final": 1.0 + nrm(ks[25], (D_MODEL,), 0.02),
    }


def reference(x_prompt, x_sample, state_ssd, state_conv, state_mlstm_c, state_mlstm_n, state_mlstm_m, meta_tokens, g_mix, w_in, conv_w, conv_b, dt_bias, a_log, d_skip, g_ssd_out, mlstm_i_bias, mlstm_f_bias, g_mlstm_out, w_out, g_ffn, peer_w_q, peer_sub_keys, peer_u, peer_v, g_final):
    weights = (g_mix, w_in, conv_w, conv_b, dt_bias, a_log, d_skip, g_ssd_out, mlstm_i_bias, mlstm_f_bias, g_mlstm_out, w_out, g_ffn, peer_w_q, peer_sub_keys, peer_u, peer_v, g_final)
    bp, sp = x_prompt.shape[0], x_prompt.shape[1]
    meta = jnp.broadcast_to(meta_tokens[None].astype(x_prompt.dtype), (bp, N_META, D_MODEL))
    hp = jnp.concatenate([meta, x_prompt], axis=1)
    z_ssd = jnp.zeros((DEPTH, bp, SSD_HEADS, SSD_HEAD_DIM, SSD_STATE), jnp.float32)
    z_conv = jnp.zeros((DEPTH, bp, CONV_W - 1, CONV_CH), x_prompt.dtype)
    z_c = jnp.zeros((DEPTH, bp, MLSTM_HEADS, MLSTM_DK, MLSTM_DV), jnp.float32)
    z_n = jnp.zeros((DEPTH, bp, MLSTM_HEADS, MLSTM_DK), jnp.float32)
    z_m = jnp.zeros((DEPTH, bp, MLSTM_HEADS), jnp.float32)
    prompt_segments = ((0, N_META, N_META), (N_META, sp, CHUNK))
    yp_full, (ps, pc, pC, pn, pm) = run_trunk(hp, z_ssd, z_conv, z_c, z_n, z_m, prompt_segments, *weights)
    y_prompt = yp_full[:, N_META:]
    ds = x_sample.shape[1]
    y_sample, (ss, sc, sC, sn, sm) = run_trunk(x_sample, state_ssd, state_conv, state_mlstm_c, state_mlstm_n, state_mlstm_m, ((0, ds, ds),), *weights)
    return (y_prompt, y_sample, ps, pc, pC, pn, pm, ss, sc, sC, sn, sm)
```

```python
import functools
import math
from typing import NamedTuple

import jax
import jax.numpy as jnp
from jax import lax
from jax.experimental import pallas as pl
from jax.experimental.pallas import tpu as pltpu

F32 = jnp.float32
BF16 = jnp.bfloat16
HIGHEST = lax.Precision.HIGHEST

LANES = 128
SUBLANES = 8
V7X_VMEM_LIMIT_BYTES = 56 * 1024 * 1024

CHUNK = 128
PEER_TOPK = 16
RMS_EPS = 1e-6
NT_DIMS = (((1,), (1,)), ((), ()))
TN_DIMS = (((0,), (0,)), ((), ()))


class Dims(NamedTuple):
    d_model: int
    batch: int
    seq: int
    dec_batch: int
    n_meta: int
    ssd_heads: int
    ssd_p: int
    ssd_n: int
    ssd_groups: int
    conv_w: int
    ml_heads: int
    ml_dk: int
    ml_dv: int
    peer_heads: int
    peer_keys: int
    peer_half: int
    n_experts: int

    @property
    def ssd_width(self):
        return self.ssd_heads * self.ssd_p

    @property
    def conv_ch(self):
        return self.ssd_width + 2 * self.ssd_groups * self.ssd_n

    @property
    def ml_width(self):
        return self.ml_heads * self.ml_dv

    @property
    def ml_qk(self):
        return self.ml_heads * self.ml_dk

    @property
    def hpg(self):
        return self.ssd_heads // self.ssd_groups

    @property
    def rows_main(self):
        return self.batch * self.seq

    @property
    def rows_sample(self):
        return _round_up(self.dec_batch, CHUNK)

    @property
    def rows_meta(self):
        return CHUNK

    @property
    def rows(self):
        return self.rows_main + self.rows_sample + self.rows_meta

    @property
    def off_xbc(self):
        return 0

    @property
    def off_q(self):
        return self.conv_ch

    @property
    def off_k(self):
        return self.off_q + self.ml_qk

    @property
    def off_z(self):
        return self.off_k + self.ml_qk

    @property
    def off_og(self):
        return self.off_z + self.ssd_width

    @property
    def off_v(self):
        return self.off_og + self.ml_width

    @property
    def off_dt(self):
        return self.off_v + self.ml_width

    @property
    def off_ig(self):
        return self.off_dt + LANES

    @property
    def off_fg(self):
        return self.off_ig + LANES

    @property
    def proj_cols(self):
        return self.off_fg + LANES


def _round_up(x, m):
    return (x + m - 1) // m * m


def _pick_tile(n, target, quantum):
    best = None
    t = quantum
    while t <= min(n, target):
        if n % t == 0:
            best = t
        t += quantum
    return best if best is not None else n


def _cparams(*sem):
    return pltpu.CompilerParams(dimension_semantics=sem, vmem_limit_bytes=V7X_VMEM_LIMIT_BYTES)


def _sigmoid(x):
    return 1.0 / (1.0 + jnp.exp(-x))


def _softplus(x):
    return jnp.maximum(x, 0.0) + jnp.log1p(jnp.exp(-jnp.abs(x)))


def _rms_scale(x, width):
    return lax.rsqrt(jnp.sum(x * x, axis=-1, keepdims=True) * (1.0 / width) + RMS_EPS)


def _head_expand(n_heads, per_head, total_rows=LANES):
    row = lax.broadcasted_iota(jnp.int32, (total_rows, n_heads * per_head), 0)
    lane = lax.broadcasted_iota(jnp.int32, (total_rows, n_heads * per_head), 1)
    return jnp.where((lane >= row * per_head) & (lane < (row + 1) * per_head), 1.0, 0.0).astype(F32)


def _expand(x, onehot):
    return jnp.dot(x, onehot, precision=HIGHEST, preferred_element_type=F32)


def _pad_rows(x, rows):
    if x.shape[0] == rows:
        return x
    return jnp.concatenate([x, jnp.zeros((rows - x.shape[0], x.shape[1]), x.dtype)], axis=0)


def _norm_matmul_kernel(x_ref, g_ref, w_ref, o_ref, hn_ref):
    @pl.when(pl.program_id(1) == 0)
    def _():
        x = x_ref[...]
        hn_ref[...] = (x * _rms_scale(x, x.shape[-1]) * g_ref[...]).astype(hn_ref.dtype)

    o_ref[...] = jnp.dot(hn_ref[...], w_ref[...], preferred_element_type=F32)


def _norm_matmul(x, gain, w):
    m, k = x.shape
    n = w.shape[1]
    tm = _pick_tile(m, 768, CHUNK)
    tn = _pick_tile(n, 1792, LANES)
    return pl.pallas_call(
        _norm_matmul_kernel,
        grid=(m // tm, n // tn),
        in_specs=[
            pl.BlockSpec((tm, k), lambda i, j: (i, 0)),
            pl.BlockSpec((1, k), lambda i, j: (0, 0)),
            pl.BlockSpec((k, tn), lambda i, j: (0, j)),
        ],
        out_specs=[
            pl.BlockSpec((tm, tn), lambda i, j: (i, j)),
            pl.BlockSpec((tm, k), lambda i, j: (i, 0)),
        ],
        out_shape=[jax.ShapeDtypeStruct((m, n), F32), jax.ShapeDtypeStruct((m, k), BF16)],
        compiler_params=_cparams("parallel", "arbitrary"),
        name="norm_matmul",
    )(x, gain.reshape(1, k), w)


def _matmul_res_kernel(a_ref, w_ref, r_ref, o_ref):
    o_ref[...] = r_ref[...] + jnp.dot(a_ref[...], w_ref[...], preferred_element_type=F32)


def _matmul_residual(a, w, res):
    m, k = a.shape
    n = w.shape[1]
    tm = _pick_tile(m, 768, CHUNK)
    tn = _pick_tile(n, 512, LANES)
    return pl.pallas_call(
        _matmul_res_kernel,
        grid=(m // tm, n // tn),
        in_specs=[
            pl.BlockSpec((tm, k), lambda i, j: (i, 0)),
            pl.BlockSpec((k, tn), lambda i, j: (0, j)),
            pl.BlockSpec((tm, tn), lambda i, j: (i, j)),
        ],
        out_specs=pl.BlockSpec((tm, tn), lambda i, j: (i, j)),
        out_shape=jax.ShapeDtypeStruct((m, n), F32),
        compiler_params=_cparams("parallel", "arbitrary"),
        name="matmul_residual",
    )(a, w, res)


def _rmsnorm_kernel(x_ref, g_ref, o_ref):
    x = x_ref[...]
    o_ref[...] = x * _rms_scale(x, x.shape[-1]) * g_ref[...]


def _rmsnorm(x, gain):
    m, k = x.shape
    tm = _pick_tile(m, 512, CHUNK)
    return pl.pallas_call(
        _rmsnorm_kernel,
        grid=(m // tm,),
        in_specs=[pl.BlockSpec((tm, k), lambda i: (i, 0)), pl.BlockSpec((1, k), lambda i: (0, 0))],
        out_specs=pl.BlockSpec((tm, k), lambda i: (i, 0)),
        out_shape=jax.ShapeDtypeStruct((m, k), F32),
        compiler_params=_cparams("parallel"),
        name="final_rmsnorm",
    )(x, gain.reshape(1, k))


def _scan_kernel(proj_ref, cw_ref, cb_ref, prm_ref, dsk_ref, gss_ref, gml_ref,
                 ycat_ref, ymeta_ref, ssd_ref, conv_ref, cst_ref, nst_ref, mst_ref,
                 st_scr, tail_scr, cbuf_scr, c_scr, n_scr, m_scr, ybuf_scr, *, d: Dims):
    b = pl.program_id(0)
    c = pl.program_id(1)
    L = CHUNK
    H, P, N, G = d.ssd_heads, d.ssd_p, d.ssd_n, d.ssd_groups
    SW, CC, GP = d.ssd_width, d.conv_ch, d.hpg * d.ssd_p
    MH, DK, DV = d.ml_heads, d.ml_dk, d.ml_dv

    @pl.when(c == 0)
    def _init():
        st_scr[...] = jnp.zeros_like(st_scr)
        tail_scr[...] = jnp.zeros_like(tail_scr)
        c_scr[...] = jnp.zeros_like(c_scr)
        n_scr[...] = jnp.zeros_like(n_scr)
        m_scr[...] = jnp.zeros_like(m_scr)

    rowi = lax.broadcasted_iota(jnp.int32, (L, 1), 0)
    lo = b * d.n_meta
    valid = (c > 0) | ((rowi >= lo) & (rowi < lo + d.n_meta))

    xp = jnp.where(valid, proj_ref[:, d.off_xbc:d.off_xbc + CC], 0.0)
    cbuf_scr[0:SUBLANES, :] = tail_scr[...]
    cbuf_scr[SUBLANES:SUBLANES + L, :] = xp
    acc = cb_ref[...] + cw_ref[3:4, :] * xp
    for kk in range(d.conv_w - 1):
        shift = d.conv_w - 1 - kk
        acc = acc + cw_ref[kk:kk + 1, :] * cbuf_scr[SUBLANES - shift:SUBLANES - shift + L, :]
    xbc = acc * _sigmoid(acc)
    ts = pl.multiple_of(jnp.where(c == 0, lo + d.n_meta, L), SUBLANES)
    tail_scr[...] = cbuf_scr[pl.ds(ts, SUBLANES), :]
    xs = xbc[:, :SW]
    bm = xbc[:, SW:SW + G * N]
    cm = xbc[:, SW + G * N:]

    lane = lax.broadcasted_iota(jnp.int32, (1, LANES), 1)
    is_h = lane < H
    is_m = lane < MH
    dt = jnp.where(valid & is_h, _softplus(proj_ref[:, d.off_dt:d.off_dt + LANES] + prm_ref[0:1, :]), 0.0)
    a_neg = jnp.where(is_h, -jnp.exp(prm_ref[1:2, :]), 0.0)
    li = jnp.where(valid, proj_ref[:, d.off_ig:d.off_ig + LANES] + prm_ref[2:3, :], -jnp.inf)
    lf = jnp.where(valid & is_m, -_softplus(-(proj_ref[:, d.off_fg:d.off_fg + LANES] + prm_ref[3:4, :])), 0.0)
    r2 = lax.broadcasted_iota(jnp.int32, (L, L), 0)
    c2 = lax.broadcasted_iota(jnp.int32, (L, L), 1)
    tril = c2 <= r2
    tri = jnp.where(tril, 1.0, 0.0).astype(F32)
    cums = jnp.dot(tri, jnp.concatenate([dt * a_neg, lf], axis=1), precision=HIGHEST,
                   preferred_element_type=F32)
    cum = cums[:, :LANES]
    bc = cums[:, LANES:]
    cum_t = cum.T
    bc_t = bc.T
    li_t = li.T
    cum_last = cum[L - 1:L, :]

    e_ssd = _head_expand(H, P)
    dt_e = _expand(dt, e_ssd)
    ecum_e = _expand(jnp.where(is_h, jnp.exp(cum), 0.0), e_ssd)
    dend_e = _expand(jnp.where(is_h, jnp.exp(cum_last - cum), 0.0), e_ssd)
    elast_e = _expand(jnp.broadcast_to(jnp.where(is_h, jnp.exp(cum_last), 0.0), (SUBLANES, LANES)), e_ssd)[0:1, :]

    xdt = xs * dt_e
    xdt_b = xdt.astype(BF16)
    xdd_b = (xdt * dend_e).astype(BF16)
    y_parts = []
    for g in range(G):
        cg = cm[:, g * N:(g + 1) * N].astype(BF16)
        bg = bm[:, g * N:(g + 1) * N].astype(BF16)
        cb = lax.dot_general(cg, bg, NT_DIMS, preferred_element_type=F32)
        sg = st_scr[:, g * GP:(g + 1) * GP]
        ys = jnp.dot(cg, sg.astype(BF16), preferred_element_type=F32)
        for hh in range(d.hpg):
            h = g * d.hpg + hh
            seg = cum[:, h:h + 1] - cum_t[h:h + 1, :]
            mat = (cb * jnp.exp(jnp.where(tril, seg, -jnp.inf))).astype(BF16)
            y_parts.append(jnp.dot(mat, xdt_b[:, h * P:(h + 1) * P], preferred_element_type=F32)
                           + ys[:, hh * P:(hh + 1) * P] * ecum_e[:, h * P:(h + 1) * P])
        st_scr[:, g * GP:(g + 1) * GP] = sg * elast_e[:, g * GP:(g + 1) * GP] + lax.dot_general(
            bg, xdd_b[:, g * GP:(g + 1) * GP], TN_DIMS, preferred_element_type=F32)
    y = jnp.concatenate(y_parts, axis=1) + dsk_ref[...] * xs
    z = proj_ref[:, d.off_z:d.off_z + SW]
    yz = y * (z * _sigmoid(z))
    ybuf_scr[:, :SW] = (yz * _rms_scale(yz, SW) * gss_ref[...]).astype(BF16)

    for hh in range(MH):
        qh = proj_ref[:, d.off_q + hh * DK:d.off_q + (hh + 1) * DK]
        kh = proj_ref[:, d.off_k + hh * DK:d.off_k + (hh + 1) * DK] * (DK ** -0.5)
        vh = proj_ref[:, d.off_v + hh * DV:d.off_v + (hh + 1) * DV]
        qb, kb, vb = qh.astype(BF16), kh.astype(BF16), vh.astype(BF16)
        bcol = bc[:, hh:hh + 1]
        brow = bc_t[hh:hh + 1, :]
        m_prev = m_scr[hh:hh + 1, 0:1]
        dm = jnp.where(tril, bcol - brow + li_t[hh:hh + 1, :], -jnp.inf)
        inter = bcol + m_prev
        m_t = jnp.maximum(inter, jnp.max(dm, axis=1, keepdims=True))
        w_inter = jnp.exp(inter - m_t)
        qkw = lax.dot_general(qb, kb, NT_DIMS, preferred_element_type=F32) * jnp.exp(dm - m_t)
        c_prev = c_scr[hh]
        n_prev = n_scr[hh:hh + 1, :]
        num = (jnp.dot(qkw.astype(BF16), vb, preferred_element_type=F32)
               + w_inter * jnp.dot(qb, c_prev.astype(BF16), preferred_element_type=F32))
        den = (jnp.sum(qkw, axis=1, keepdims=True)
               + w_inter * jnp.sum(qh * n_prev, axis=1, keepdims=True))
        hv = num / jnp.maximum(jnp.abs(den), jnp.exp(-m_t))
        b_last = bcol[L - 1:L, :]
        end_inter = b_last + m_prev
        m_new = jnp.maximum(end_inter, jnp.max(b_last - brow + li_t[hh:hh + 1, :], axis=1, keepdims=True))
        we = jnp.exp(b_last - bcol + li[:, hh:hh + 1] - m_new)
        carry = jnp.exp(end_inter - m_new)
        kw = kh * we
        c_scr[hh] = carry * c_prev + lax.dot_general(kw.astype(BF16), vb, TN_DIMS, preferred_element_type=F32)
        n_scr[hh:hh + 1, :] = carry * n_prev + jnp.sum(kw, axis=0, keepdims=True)
        m_scr[hh:hh + 1, :] = jnp.broadcast_to(m_new, (1, LANES))
        og = proj_ref[:, d.off_og + hh * DV:d.off_og + (hh + 1) * DV]
        hm = hv * _rms_scale(hv, DV) * gml_ref[:, hh * DV:(hh + 1) * DV] * _sigmoid(og)
        ybuf_scr[:, SW + hh * DV:SW + (hh + 1) * DV] = hm.astype(BF16)

    @pl.when(c > 0)
    def _():
        ycat_ref[...] = ybuf_scr[...]

    @pl.when(c == 0)
    def _():
        ymeta_ref[0] = ybuf_scr[pl.ds(pl.multiple_of(lo, d.n_meta), d.n_meta), :]

    @pl.when(c == pl.num_programs(1) - 1)
    def _():
        ssd_ref[0] = st_scr[...].T.reshape(H, P, N)
        conv_ref[0] = tail_scr[...]
        cst_ref[0] = c_scr[...]
        nst_ref[0] = n_scr[...]
        mst_ref[0] = m_scr[...]


def _prompt_scan(proj, lw, d: Dims):
    nchunk = d.seq // CHUNK
    meta_blk = (d.rows_main + d.rows_sample) // CHUNK
    width = d.ssd_width + d.ml_width

    def row_map(b, c):
        return (jnp.where(c == 0, meta_blk, b * nchunk + c - 1), 0)

    def out_map(b, c):
        return (b * nchunk + jnp.maximum(c - 1, 0), 0)

    const2 = lambda b, c: (0, 0)
    n_sub = _round_up(d.ml_heads, SUBLANES)
    return pl.pallas_call(
        functools.partial(_scan_kernel, d=d),
        grid=(d.batch, nchunk + 1),
        in_specs=[
            pl.BlockSpec((CHUNK, d.proj_cols), row_map),
            pl.BlockSpec((d.conv_w, d.conv_ch), const2),
            pl.BlockSpec((1, d.conv_ch), const2),
            pl.BlockSpec((SUBLANES, LANES), const2),
            pl.BlockSpec((1, d.ssd_width), const2),
            pl.BlockSpec((1, d.ssd_width), const2),
            pl.BlockSpec((1, d.ml_width), const2),
        ],
        out_specs=[
            pl.BlockSpec((CHUNK, width), out_map),
            pl.BlockSpec((1, d.n_meta, width), lambda b, c: (b, 0, 0)),
            pl.BlockSpec((1, d.ssd_heads, d.ssd_p, d.ssd_n), lambda b, c: (b, 0, 0, 0)),
            pl.BlockSpec((1, SUBLANES, d.conv_ch), lambda b, c: (b, 0, 0)),
            pl.BlockSpec((1, d.ml_heads, d.ml_dk, d.ml_dv), lambda b, c: (b, 0, 0, 0)),
            pl.BlockSpec((1, n_sub, d.ml_dk), lambda b, c: (b, 0, 0)),
            pl.BlockSpec((1, n_sub, LANES), lambda b, c: (b, 0, 0)),
        ],
        out_shape=[
            jax.ShapeDtypeStruct((d.rows_main, width), BF16),
            jax.ShapeDtypeStruct((d.batch, d.n_meta, width), BF16),
            jax.ShapeDtypeStruct((d.batch, d.ssd_heads, d.ssd_p, d.ssd_n), F32),
            jax.ShapeDtypeStruct((d.batch, SUBLANES, d.conv_ch), F32),
            jax.ShapeDtypeStruct((d.batch, d.ml_heads, d.ml_dk, d.ml_dv), F32),
            jax.ShapeDtypeStruct((d.batch, n_sub, d.ml_dk), F32),
            jax.ShapeDtypeStruct((d.batch, n_sub, LANES), F32),
        ],
        scratch_shapes=[
            pltpu.VMEM((d.ssd_n, d.ssd_width), F32),
            pltpu.VMEM((SUBLANES, d.conv_ch), F32),
            pltpu.VMEM((SUBLANES + CHUNK, d.conv_ch), F32),
            pltpu.VMEM((d.ml_heads, d.ml_dk, d.ml_dv), F32),
            pltpu.VMEM((n_sub, d.ml_dk), F32),
            pltpu.VMEM((n_sub, LANES), F32),
            pltpu.VMEM((CHUNK, width), BF16),
        ],
        compiler_params=_cparams("parallel", "arbitrary"),
        name="prompt_scan",
    )(proj, lw["conv_w"], lw["conv_b"], lw["prm"], lw["d_skip_e"], lw["g_ssd"], lw["g_ml"])


SAMPLE_BT = SUBLANES


def _sample_ssd_kernel(proj_ref, hist_ref, sst_ref, cw_ref, cb_ref, prm_ref, dsk_ref, gss_ref,
                       y_ref, ossd_ref, oconv_ref, *, d: Dims):
    H, P, N, G = d.ssd_heads, d.ssd_p, d.ssd_n, d.ssd_groups
    SW, CC, GP = d.ssd_width, d.conv_ch, d.hpg * d.ssd_p
    bt = SAMPLE_BT
    r0 = pl.multiple_of((pl.program_id(0) * bt) % CHUNK, bt)
    xp = proj_ref[pl.ds(r0, bt), d.off_xbc:d.off_xbc + CC]
    acc = cb_ref[...] + cw_ref[3:4, :] * xp
    for kk in range(d.conv_w - 1):
        acc = acc + cw_ref[kk:kk + 1, :] * hist_ref[kk]
    for kk in range(d.conv_w - 2):
        oconv_ref[kk] = hist_ref[kk + 1]
    oconv_ref[d.conv_w - 2] = xp
    xbc = acc * _sigmoid(acc)
    xs = xbc[:, :SW]
    bm = xbc[:, SW:SW + G * N]
    cm = xbc[:, SW + G * N:]

    lane = lax.broadcasted_iota(jnp.int32, (1, LANES), 1)
    is_h = lane < H
    dt = jnp.where(is_h, _softplus(proj_ref[pl.ds(r0, bt), d.off_dt:d.off_dt + LANES] + prm_ref[0:1, :]), 0.0)
    dec = jnp.where(is_h, jnp.exp(dt * -jnp.exp(prm_ref[1:2, :])), 0.0)
    e_ssd = _head_expand(H, P)
    xdt_t = _pad_rows(xs * _expand(dt, e_ssd), LANES).T
    dec_t = _pad_rows(_expand(dec, e_ssd), LANES).T
    lane_full = lax.broadcasted_iota(jnp.int32, (SW, LANES), 1)
    y_t = jnp.zeros((SW, LANES), F32)
    for i in range(bt):
        st = sst_ref[i].reshape(SW, N)
        b_full = jnp.concatenate(
            [jnp.broadcast_to(bm[i:i + 1, g * N:(g + 1) * N], (GP, N)) for g in range(G)], axis=0)
        c_full = jnp.concatenate(
            [jnp.broadcast_to(cm[i:i + 1, g * N:(g + 1) * N], (GP, N)) for g in range(G)], axis=0)
        new = st * dec_t[:, i:i + 1] + xdt_t[:, i:i + 1] * b_full
        ossd_ref[i] = new.reshape(H, P, N)
        y_t = jnp.where(lane_full == i, jnp.sum(new * c_full, axis=1, keepdims=True), y_t)
    y = y_t.T[:bt, :] + dsk_ref[...] * xs
    z = proj_ref[pl.ds(r0, bt), d.off_z:d.off_z + SW]
    yz = y * (z * _sigmoid(z))
    y_ref[...] = (yz * _rms_scale(yz, SW) * gss_ref[...]).astype(y_ref.dtype)


def _sample_ssd(proj, hist, sst, lw, d: Dims):
    bt = SAMPLE_BT
    base_blk = d.rows_main // CHUNK
    const2 = lambda i: (0, 0)
    return pl.pallas_call(
        functools.partial(_sample_ssd_kernel, d=d),
        grid=(d.dec_batch // bt,),
        in_specs=[
            pl.BlockSpec((CHUNK, d.proj_cols), lambda i: (base_blk + (i * bt) // CHUNK, 0)),
            pl.BlockSpec((d.conv_w - 1, bt, d.conv_ch), lambda i: (0, i, 0)),
            pl.BlockSpec((bt, d.ssd_heads, d.ssd_p, d.ssd_n), lambda i: (i, 0, 0, 0)),
            pl.BlockSpec((d.conv_w, d.conv_ch), const2),
            pl.BlockSpec((1, d.conv_ch), const2),
            pl.BlockSpec((SUBLANES, LANES), const2),
            pl.BlockSpec((1, d.ssd_width), const2),
            pl.BlockSpec((1, d.ssd_width), const2),
        ],
        out_specs=[
            pl.BlockSpec((bt, d.ssd_width), lambda i: (i, 0)),
            pl.BlockSpec((bt, d.ssd_heads, d.ssd_p, d.ssd_n), lambda i: (i, 0, 0, 0)),
            pl.BlockSpec((d.conv_w - 1, bt, d.conv_ch), lambda i: (0, i, 0)),
        ],
        out_shape=[
            jax.ShapeDtypeStruct((d.dec_batch, d.ssd_width), F32),
            jax.ShapeDtypeStruct((d.dec_batch, d.ssd_heads, d.ssd_p, d.ssd_n), F32),
            jax.ShapeDtypeStruct((d.conv_w - 1, d.dec_batch, d.conv_ch), F32),
        ],
        compiler_params=_cparams("parallel"),
        name="sample_ssd",
    )(proj, hist, sst, lw["conv_w"], lw["conv_b"], lw["prm"], lw["d_skip_e"], lw["g_ssd"])


def _sample_mlstm_kernel(proj_ref, cst_ref, nst_ref, mst_ref, prm_ref, gml_ref,
                         y_ref, oc_ref, on_ref, om_ref, *, d: Dims):
    MH, DK, DV = d.ml_heads, d.ml_dk, d.ml_dv
    QK, MW = d.ml_qk, d.ml_width
    bt = SAMPLE_BT
    r0 = pl.multiple_of((pl.program_id(0) * bt) % CHUNK, bt)
    rows = pl.ds(r0, bt)
    lane = lax.broadcasted_iota(jnp.int32, (1, LANES), 1)
    is_m = lane < MH
    li = proj_ref[rows, d.off_ig:d.off_ig + LANES] + prm_ref[2:3, :]
    lf = -_softplus(-(proj_ref[rows, d.off_fg:d.off_fg + LANES] + prm_ref[3:4, :]))
    inter = lf + mst_ref[...]
    m_t = jnp.maximum(inter, li)
    w_intra = jnp.where(is_m, jnp.exp(li - m_t), 0.0)
    w_inter = jnp.where(is_m, jnp.exp(inter - m_t), 0.0)
    om_ref[...] = m_t

    e_qk = _head_expand(MH, DK)
    e_v = _head_expand(MH, DV)
    q = proj_ref[rows, d.off_q:d.off_q + QK]
    k = proj_ref[rows, d.off_k:d.off_k + QK] * (DK ** -0.5)
    v = proj_ref[rows, d.off_v:d.off_v + MW]
    wi_e = _expand(w_intra, e_qk)
    wn_e = _expand(w_inter, e_qk)
    kw = k * wi_e
    n_new = wn_e * nst_ref[...] + kw
    on_ref[...] = n_new
    den = lax.dot_general(q * n_new, e_qk, NT_DIMS, precision=HIGHEST, preferred_element_type=F32)
    den = jnp.maximum(jnp.abs(den), jnp.exp(-m_t))
    kw_t = _pad_rows(kw, LANES).T
    q_t = _pad_rows(q, LANES).T
    wn_t = _pad_rows(wn_e, LANES).T
    num_rows = []
    for i in range(bt):
        parts = []
        for hh in range(MH):
            rs = slice(hh * DK, (hh + 1) * DK)
            c_new = cst_ref[i, hh] * wn_t[rs, i:i + 1] + kw_t[rs, i:i + 1] * v[i:i + 1, hh * DV:(hh + 1) * DV]
            oc_ref[i, hh] = c_new
            parts.append(jnp.sum(c_new * q_t[rs, i:i + 1], axis=0, keepdims=True))
        num_rows.append(jnp.concatenate(parts, axis=1))
    hv = jnp.concatenate(num_rows, axis=0) / _expand(den, e_v)
    ms = lax.dot_general(hv * hv, e_v, NT_DIMS, precision=HIGHEST, preferred_element_type=F32) * (1.0 / DV)
    og = proj_ref[rows, d.off_og:d.off_og + MW]
    y_ref[...] = (hv * _expand(lax.rsqrt(ms + RMS_EPS), e_v) * gml_ref[...] * _sigmoid(og)).astype(y_ref.dtype)


def _sample_mlstm(proj, cst, nst, mst, lw, d: Dims):
    bt = SAMPLE_BT
    base_blk = d.rows_main // CHUNK
    const2 = lambda i: (0, 0)
    return pl.pallas_call(
        functools.partial(_sample_mlstm_kernel, d=d),
        grid=(d.dec_batch // bt,),
        in_specs=[
            pl.BlockSpec((CHUNK, d.proj_cols), lambda i: (base_blk + (i * bt) // CHUNK, 0)),
            pl.BlockSpec((bt, d.ml_heads, d.ml_dk, d.ml_dv), lambda i: (i, 0, 0, 0)),
            pl.BlockSpec((bt, d.ml_qk), lambda i: (i, 0)),
            pl.BlockSpec((bt, LANES), lambda i: (i, 0)),
            pl.BlockSpec((SUBLANES, LANES), const2),
            pl.BlockSpec((1, d.ml_width), const2),
        ],
        out_specs=[
            pl.BlockSpec((bt, d.ml_width), lambda i: (i, 0)),
            pl.BlockSpec((bt, d.ml_heads, d.ml_dk, d.ml_dv), lambda i: (i, 0, 0, 0)),
            pl.BlockSpec((bt, d.ml_qk), lambda i: (i, 0)),
            pl.BlockSpec((bt, LANES), lambda i: (i, 0)),
        ],
        out_shape=[
            jax.ShapeDtypeStruct((d.dec_batch, d.ml_width), F32),
            jax.ShapeDtypeStruct((d.dec_batch, d.ml_heads, d.ml_dk, d.ml_dv), F32),
            jax.ShapeDtypeStruct((d.dec_batch, d.ml_qk), F32),
            jax.ShapeDtypeStruct((d.dec_batch, LANES), F32),
        ],
        compiler_params=_cparams("parallel"),
        name="sample_mlstm",
    )(proj, cst, nst, mst, lw["prm"], lw["g_ml"])


def _topk_rows(s, k):
    n = s.shape[0]
    row = lax.broadcasted_iota(jnp.int32, s.shape, 0)
    vals, idxs = [], []
    for _ in range(k):
        m = jnp.max(s, axis=0, keepdims=True)
        idx = jnp.min(jnp.where(s == m, row, n), axis=0, keepdims=True)
        vals.append(m)
        idxs.append(idx)
        s = jnp.where(row == idx, -jnp.inf, s)
    return jnp.concatenate(vals, axis=0), jnp.concatenate(idxs, axis=0)


def _select_rows(sel, table, k):
    out = jnp.zeros(sel.shape, table.dtype)
    for a in range(k):
        out = jnp.where(sel == a, table[a:a + 1, :], out)
    return out


def _route_kernel(q_ref, sk_ref, r_ref, c_ref, g_ref, *, heads, k):
    tt = q_ref.shape[0]
    half = sk_ref.shape[3]
    r_parts, c_parts, g_parts = [], [], []
    for h in range(heads):
        tops = []
        for side in range(2):
            col = (2 * h + side) * half
            s = lax.dot_general(sk_ref[side, h], q_ref[:, col:col + half], NT_DIMS,
                                precision=HIGHEST, preferred_element_type=F32)
            tops.append(_topk_rows(s, k))
        (v1, i1), (v2, i2) = tops
        cand = jnp.concatenate([v1[a:a + 1, :] + v2 for a in range(k)], axis=0)
        sc, ci = _topk_rows(cand, k)
        r_parts.append(_select_rows(lax.shift_right_logical(ci, k.bit_length() - 1), i1, k))
        c_parts.append(_select_rows(ci & (k - 1), i2, k))
        e = jnp.exp(sc - sc[0:1, :])
        g_parts.append(e / jnp.sum(e, axis=0, keepdims=True))
    pad = LANES - heads * k
    if pad:
        r_parts.append(jnp.zeros((pad, tt), jnp.int32))
        c_parts.append(jnp.zeros((pad, tt), jnp.int32))
        g_parts.append(jnp.zeros((pad, tt), F32))
    r_ref[...] = jnp.concatenate(r_parts, axis=0).astype(F32).T.astype(jnp.int32)
    c_ref[...] = jnp.concatenate(c_parts, axis=0).astype(F32).T.astype(jnp.int32)
    g_ref[...] = jnp.concatenate(g_parts, axis=0).T


def _peer_route(q, sub_keys, d: Dims):
    m = q.shape[0]
    tt = LANES
    spec = pl.BlockSpec((tt, LANES), lambda i: (i, 0))
    return pl.pallas_call(
        functools.partial(_route_kernel, heads=d.peer_heads, k=PEER_TOPK),
        grid=(m // tt,),
        in_specs=[
            pl.BlockSpec((tt, q.shape[1]), lambda i: (i, 0)),
            pl.BlockSpec(sub_keys.shape, lambda i: (0, 0, 0, 0)),
        ],
        out_specs=[spec, spec, spec],
        out_shape=[jax.ShapeDtypeStruct((m, LANES), jnp.int32), jax.ShapeDtypeStruct((m, LANES), jnp.int32),
                   jax.ShapeDtypeStruct((m, LANES), F32)],
        compiler_params=_cparams("parallel"),
        name="peer_route",
    )(q, sub_keys)


GATE_TT = 64


def _gate_kernel(r_ref, c_ref, g_ref, w_ref, ghi_scr, glo_scr):
    tt = r_ref.shape[0]
    g = g_ref[...]
    g_hi = g.astype(BF16).astype(F32)
    ghi_scr[...] = g_hi
    glo_scr[...] = g - g_hi
    sub = lax.broadcasted_iota(jnp.int32, (LANES, LANES), 0)

    def per_group(tg, carry):
        base = pl.multiple_of(tg * SUBLANES, SUBLANES)
        for tl in range(SUBLANES):
            row = pl.ds(base + tl, 1)
            on_r = jnp.broadcast_to(r_ref[row, :], (LANES, LANES)) == sub
            on_c = jnp.broadcast_to(c_ref[row, :], (LANES, LANES)) == sub
            hi = jnp.broadcast_to(ghi_scr[row, :], (LANES, LANES))
            lo = jnp.broadcast_to(glo_scr[row, :], (LANES, LANES))
            lhs = jnp.concatenate([jnp.where(on_r, hi, 0.0), jnp.where(on_r, lo, 0.0)], axis=1).astype(BF16)
            one = jnp.where(on_c, 1.0, 0.0).astype(BF16)
            rhs = jnp.concatenate([one, one], axis=1)
            w_t = lax.dot_general(lhs, rhs, NT_DIMS, preferred_element_type=F32)
            w_ref[tg, pl.ds(tl, LANES, stride=SUBLANES), :] = w_t
        return carry

    lax.fori_loop(0, tt // SUBLANES, per_group, 0)


def _peer_gates(r_idx, c_idx, gate):
    m = r_idx.shape[0]
    tt = GATE_TT
    spec = pl.BlockSpec((tt, LANES), lambda i: (i, 0))
    return pl.pallas_call(
        _gate_kernel,
        grid=(m // tt,),
        in_specs=[spec, spec, spec],
        out_specs=pl.BlockSpec((tt // SUBLANES, LANES * SUBLANES, LANES), lambda i: (i, 0, 0)),
        out_shape=jax.ShapeDtypeStruct((m // SUBLANES, LANES * SUBLANES, LANES), F32),
        scratch_shapes=[pltpu.VMEM((tt, LANES), F32), pltpu.VMEM((tt, LANES), F32)],
        compiler_params=_cparams("parallel"),
        name="peer_gates",
    )(r_idx, c_idx, gate)


def _gelu(x):
    return 0.5 * x * (1.0 + lax.erf(x * math.sqrt(0.5)))


def _peer_kernel(x_ref, u_ref, v_ref, w_ref, h_ref, o_ref, acc_ref):
    j = pl.program_id(1)
    tt = x_ref.shape[0]
    ce = u_ref.shape[0]

    @pl.when(j == 0)
    def _():
        acc_ref[...] = jnp.zeros_like(acc_ref)

    a = lax.dot_general(x_ref[...], u_ref[...], NT_DIMS, preferred_element_type=F32)
    w = jnp.concatenate(
        [w_ref[:, r * SUBLANES:(r + 1) * SUBLANES, :].reshape(tt, LANES) for r in range(ce // LANES)], axis=1)
    acc_ref[...] += jnp.dot((_gelu(a) * w).astype(BF16), v_ref[...], preferred_element_type=F32)

    @pl.when(j == pl.num_programs(1) - 1)
    def _():
        o_ref[...] = h_ref[...] + acc_ref[...]


def _peer_experts(x, u, v, w, h):
    m, k = x.shape
    e = u.shape[0]
    tt = _pick_tile(m, 512, CHUNK)
    ce = _pick_tile(e, 512, LANES)
    rows_spec = pl.BlockSpec((tt, k), lambda i, j: (i, 0))
    tab_spec = pl.BlockSpec((ce, k), lambda i, j: (j, 0))
    return pl.pallas_call(
        _peer_kernel,
        grid=(m // tt, e // ce),
        in_specs=[
            rows_spec, tab_spec, tab_spec,
            pl.BlockSpec((tt // SUBLANES, ce // LANES * SUBLANES, LANES), lambda i, j: (i, j, 0)),
            rows_spec,
        ],
        out_specs=rows_spec,
        out_shape=jax.ShapeDtypeStruct((m, k), F32),
        scratch_shapes=[pltpu.VMEM((tt, k), F32)],
        compiler_params=_cparams("parallel", "arbitrary"),
        name="peer_experts",
    )(x, u, v, w, h)


def _lane_row(x):
    return jnp.pad(x.astype(F32), (0, LANES - x.shape[0])).reshape(1, LANES)


def _pack_layer(l, d: Dims, w_in, conv_w, conv_b, dt_bias, a_log, d_skip, g_ssd_out, i_bias, f_bias,
                g_mlstm_out, w_out, peer_w_q, peer_u, peer_v):
    sizes = (d.ssd_width, d.conv_ch, d.ssd_heads, d.ml_qk, d.ml_qk, d.ml_width, d.ml_heads, d.ml_heads, d.ml_width)
    offs = [0]
    for s in sizes:
        offs.append(offs[-1] + s)
    col = lambda i: w_in[l][:, offs[i]:offs[i + 1]]
    pad_lanes = lambda w: jnp.pad(w, ((0, 0), (0, LANES - w.shape[1])))
    w_z, w_xbc, w_dt, w_q, w_k, w_v, w_ig, w_fg, w_og = [col(i) for i in range(9)]
    w_pack = jnp.concatenate(
        [w_xbc, w_q, w_k, w_z, w_og, w_v, pad_lanes(w_dt), pad_lanes(w_ig), pad_lanes(w_fg)], axis=1)
    w_pack = w_pack.astype(BF16)
    prm = jnp.concatenate(
        [_lane_row(dt_bias[l]), _lane_row(a_log[l]), _lane_row(i_bias[l]), _lane_row(f_bias[l]),
         jnp.zeros((SUBLANES - 4, LANES), F32)], axis=0)
    return dict(
        w_in=w_pack,
        conv_w=conv_w[l], conv_b=conv_b[l].reshape(1, -1), prm=prm,
        d_skip_e=jnp.repeat(d_skip[l], d.ssd_p).reshape(1, -1),
        g_ssd=g_ssd_out[l].reshape(1, -1), g_ml=g_mlstm_out[l].reshape(1, -1),
        w_out=w_out[l].astype(BF16), w_q=peer_w_q[l].astype(BF16),
        u=peer_u[l].astype(BF16), v=peer_v[l].astype(BF16),
    )


def _derive_dims(x_prompt, x_sample, state_ssd, state_conv, state_mlstm_c, meta_tokens, peer_sub_keys, peer_u):
    batch, seq, d_model = x_prompt.shape
    _, dec_batch, heads, p, n = state_ssd.shape
    conv_ch = state_conv.shape[-1]
    groups = (conv_ch - heads * p) // (2 * n)
    _, _, mh, dk, dv = state_mlstm_c.shape
    _, ph, keys, half = peer_sub_keys.shape[1:]
    d = Dims(d_model=d_model, batch=batch, seq=seq, dec_batch=dec_batch, n_meta=meta_tokens.shape[0],
             ssd_heads=heads, ssd_p=p, ssd_n=n, ssd_groups=groups, conv_w=state_conv.shape[2] + 1,
             ml_heads=mh, ml_dk=dk, ml_dv=dv, peer_heads=ph, peer_keys=keys, peer_half=half,
             n_experts=peer_u.shape[1])
    assert x_sample.shape[1] == 1 and seq % CHUNK == 0
    assert d.batch * d.n_meta <= CHUNK and d.n_meta % 16 == 0
    assert d.dec_batch % SAMPLE_BT == 0
    assert d.ssd_n == LANES and d.peer_keys == LANES and d.peer_half == LANES
    assert d.ssd_heads <= LANES and d.ml_heads <= SUBLANES and d.conv_w == 4
    assert d.peer_heads * PEER_TOPK <= LANES and d.n_experts == d.peer_keys ** 2
    assert all(w % LANES == 0 for w in (d.ssd_width, d.ml_qk, d.ml_width, d.hpg * d.ssd_p, d.ml_dk))
    return d


def kernel(x_prompt, x_sample, state_ssd, state_conv, state_mlstm_c, state_mlstm_n, state_mlstm_m, meta_tokens, g_mix, w_in, conv_w, conv_b, dt_bias, a_log, d_skip, g_ssd_out, mlstm_i_bias, mlstm_f_bias, g_mlstm_out, w_out, g_ffn, peer_w_q, peer_sub_keys, peer_u, peer_v, g_final):
    d = _derive_dims(x_prompt, x_sample, state_ssd, state_conv, state_mlstm_c, meta_tokens, peer_sub_keys, peer_u)
    depth = w_in.shape[0]
    dm = d.d_model
    meta = jnp.broadcast_to(meta_tokens[None].astype(F32), (d.batch, d.n_meta, dm)).reshape(-1, dm)
    h = jnp.concatenate([
        x_prompt.reshape(-1, dm),
        _pad_rows(x_sample.reshape(-1, dm), d.rows_sample),
        _pad_rows(meta, d.rows_meta),
    ], axis=0)

    new_p = ([], [], [], [], [])
    new_s = ([], [], [], [], [])
    for l in range(depth):
        lw = _pack_layer(l, d, w_in, conv_w, conv_b, dt_bias, a_log, d_skip, g_ssd_out, mlstm_i_bias,
                         mlstm_f_bias, g_mlstm_out, w_out, peer_w_q, peer_u, peer_v)
        proj, _ = _norm_matmul(h, g_mix[l], lw["w_in"])

        y_main, y_meta, p_ssd, p_conv, p_c, p_n, p_m = _prompt_scan(proj, lw, d)
        s_yssd, s_ssd, s_conv = _sample_ssd(proj, jnp.transpose(state_conv[l], (1, 0, 2)), state_ssd[l], lw, d)
        m_pad = jnp.pad(state_mlstm_m[l], ((0, 0), (0, LANES - d.ml_heads)))
        s_yml, s_c, s_n, s_m = _sample_mlstm(
            proj, state_mlstm_c[l], state_mlstm_n[l].reshape(d.dec_batch, d.ml_qk), m_pad, lw, d)

        y_sample = jnp.concatenate([s_yssd, s_yml], axis=1).astype(BF16)
        ycat = jnp.concatenate([
            y_main,
            _pad_rows(y_sample, d.rows_sample),
            _pad_rows(y_meta.reshape(d.batch * d.n_meta, -1), d.rows_meta),
        ], axis=0)
        h = _matmul_residual(ycat, lw["w_out"], h)

        q, xn = _norm_matmul(h, g_ffn[l], lw["w_q"])
        r_idx, c_idx, gate = _peer_route(q, peer_sub_keys[l], d)
        h = _peer_experts(xn, lw["u"], lw["v"], _peer_gates(r_idx, c_idx, gate), h)

        for lst, s in zip(new_p, (p_ssd, p_conv[:, SUBLANES - (d.conv_w - 1):, :], p_c,
                                  p_n[:, :d.ml_heads, :], p_m[:, :d.ml_heads, 0])):
            lst.append(s)
        for lst, s in zip(new_s, (s_ssd, jnp.transpose(s_conv, (1, 0, 2)), s_c,
                                  s_n.reshape(d.dec_batch, d.ml_heads, d.ml_dk), s_m[:, :d.ml_heads])):
            lst.append(s)

    y = _rmsnorm(h, g_final)
    y_prompt = y[:d.rows_main].reshape(d.batch, d.seq, dm)
    y_sample = y[d.rows_main:d.rows_main + d.dec_batch].reshape(d.dec_batch, 1, dm)
    return (y_prompt, y_sample) + tuple(jnp.stack(s) for s in new_p) + tuple(jnp.stack(s) for s in new_s)
```

```python
import functools
import math
from typing import NamedTuple

import jax
import jax.numpy as jnp
from jax import lax
from jax.experimental import pallas as pl
from jax.experimental.pallas import tpu as pltpu

F32 = jnp.float32
BF16 = jnp.bfloat16
HIGHEST = lax.Precision.HIGHEST

LANES = 128
SUBLANES = 8
V7X_VMEM_LIMIT_BYTES = 56 * 1024 * 1024

CHUNK = 128
PEER_TOPK = 16
RMS_EPS = 1e-6
NT_DIMS = (((1,), (1,)), ((), ()))
TN_DIMS = (((0,), (0,)), ((), ()))


class Dims(NamedTuple):
    d_model: int
    batch: int
    seq: int
    dec_batch: int
    n_meta: int
    ssd_heads: int
    ssd_p: int
    ssd_n: int
    ssd_groups: int
    conv_w: int
    ml_heads: int
    ml_dk: int
    ml_dv: int
    peer_heads: int
    peer_keys: int
    peer_half: int
    n_experts: int

    @property
    def ssd_width(self):
        return self.ssd_heads * self.ssd_p

    @property
    def conv_ch(self):
        return self.ssd_width + 2 * self.ssd_groups * self.ssd_n

    @property
    def ml_width(self):
        return self.ml_heads * self.ml_dv

    @property
    def ml_qk(self):
        return self.ml_heads * self.ml_dk

    @property
    def hpg(self):
        return self.ssd_heads // self.ssd_groups

    @property
    def rows_main(self):
        return self.batch * self.seq

    @property
    def rows_sample(self):
        return _round_up(self.dec_batch, CHUNK)

    @property
    def rows_meta(self):
        return CHUNK

    @property
    def rows(self):
        return self.rows_main + self.rows_sample + self.rows_meta

    @property
    def off_xbc(self):
        return 0

    @property
    def off_q(self):
        return self.conv_ch

    @property
    def off_k(self):
        return self.off_q + self.ml_qk

    @property
    def off_z(self):
        return self.off_k + self.ml_qk

    @property
    def off_og(self):
        return self.off_z + self.ssd_width

    @property
    def off_v(self):
        return self.off_og + self.ml_width

    @property
    def off_dt(self):
        return self.off_v + self.ml_width

    @property
    def off_ig(self):
        return self.off_dt + LANES

    @property
    def off_fg(self):
        return self.off_ig + LANES

    @property
    def proj_cols(self):
        return self.off_fg + LANES


def _round_up(x, m):
    return (x + m - 1) // m * m


def _pick_tile(n, target, quantum):
    best = None
    t = quantum
    while t <= min(n, target):
        if n % t == 0:
            best = t
        t += quantum
    return best if best is not None else n


def _cparams(*sem):
    return pltpu.CompilerParams(dimension_semantics=sem, vmem_limit_bytes=V7X_VMEM_LIMIT_BYTES)


def _sigmoid(x):
    return 1.0 / (1.0 + jnp.exp(-x))


def _softplus(x):
    return jnp.maximum(x, 0.0) + jnp.log1p(jnp.exp(-jnp.abs(x)))


def _rms_scale(x, width):
    return lax.rsqrt(jnp.sum(x * x, axis=-1, keepdims=True) * (1.0 / width) + RMS_EPS)


def _head_expand(n_heads, per_head, total_rows=LANES):
    row = lax.broadcasted_iota(jnp.int32, (total_rows, n_heads * per_head), 0)
    lane = lax.broadcasted_iota(jnp.int32, (total_rows, n_heads * per_head), 1)
    return jnp.where((lane >= row * per_head) & (lane < (row + 1) * per_head), 1.0, 0.0).astype(F32)


def _expand_many(xs, onehot):
    x = jnp.concatenate(xs, axis=0) if len(xs) > 1 else xs[0]
    p1 = x.astype(BF16)
    r1 = x - p1.astype(F32)
    p2 = r1.astype(BF16)
    p3 = (r1 - p2.astype(F32)).astype(BF16)
    e = onehot.astype(BF16)
    out = jnp.dot(jnp.concatenate([p1, p2, p3], axis=1), jnp.concatenate([e, e, e], axis=0),
                  preferred_element_type=F32)
    outs, start = [], 0
    for xi in xs:
        outs.append(out[start:start + xi.shape[0], :])
        start += xi.shape[0]
    return outs


def _expand(x, onehot):
    return _expand_many([x], onehot)[0]


def _pad_rows(x, rows):
    if x.shape[0] == rows:
        return x
    return jnp.concatenate([x, jnp.zeros((rows - x.shape[0], x.shape[1]), x.dtype)], axis=0)


def _norm_matmul_kernel(x_ref, g_ref, w_ref, o_ref, hn_ref):
    @pl.when(pl.program_id(1) == 0)
    def _():
        x = x_ref[...]
        hn_ref[...] = (x * _rms_scale(x, x.shape[-1]) * g_ref[...]).astype(hn_ref.dtype)

    o_ref[...] = jnp.dot(hn_ref[...], w_ref[...], preferred_element_type=F32)


def _norm_matmul(x, gain, w, l):
    m, k = x.shape
    n = w.shape[2]
    tm = _pick_tile(m, 768, CHUNK)
    tn = _pick_tile(n, 1792, LANES)
    return pl.pallas_call(
        _norm_matmul_kernel,
        grid=(m // tm, n // tn),
        in_specs=[
            pl.BlockSpec((tm, k), lambda i, j: (i, 0)),
            pl.BlockSpec((1, k), lambda i, j: (0, 0)),
            pl.BlockSpec((None, k, tn), lambda i, j: (l, 0, j)),
        ],
        out_specs=[
            pl.BlockSpec((tm, tn), lambda i, j: (i, j)),
            pl.BlockSpec((tm, k), lambda i, j: (i, 0)),
        ],
        out_shape=[jax.ShapeDtypeStruct((m, n), F32), jax.ShapeDtypeStruct((m, k), BF16)],
        compiler_params=_cparams("parallel", "arbitrary"),
        name="norm_matmul",
    )(x, gain.reshape(1, k), w)


MIX_TM = 256


def _mix_out_kernel(ymain_ref, ytail_ref, wo_ref, res_ref, g_ref, wq_ref, h_ref, xn_ref, q_ref, *, n_main):
    i = pl.program_id(0)

    def finish(a):
        h1 = res_ref[...] + jnp.dot(a, wo_ref[...], preferred_element_type=F32)
        h_ref[...] = h1
        xn = (h1 * _rms_scale(h1, h1.shape[-1]) * g_ref[...]).astype(BF16)
        xn_ref[...] = xn
        q_ref[...] = jnp.dot(xn, wq_ref[...], preferred_element_type=F32)

    @pl.when(i < n_main)
    def _():
        finish(ymain_ref[...])

    @pl.when(i >= n_main)
    def _():
        finish(ytail_ref[...])


def _mix_out(y_main, y_tail, w_out, res, gain, w_q, l):
    m, dm = res.shape
    k = y_main.shape[1]
    nq = w_q.shape[2]
    tm = MIX_TM
    n_main = y_main.shape[0] // tm
    n_tail = y_tail.shape[0] // tm
    rows = lambda width: pl.BlockSpec((tm, width), lambda i: (i, 0))
    resident = lambda shape: pl.BlockSpec(shape, lambda i: (l, 0, 0), pipeline_mode=pl.Buffered(1))
    return pl.pallas_call(
        functools.partial(_mix_out_kernel, n_main=n_main),
        grid=(n_main + n_tail,),
        in_specs=[
            pl.BlockSpec((tm, k), lambda i: (jnp.minimum(i, n_main - 1), 0)),
            pl.BlockSpec((tm, k), lambda i: (jnp.maximum(i - n_main, 0), 0)),
            resident((None, k, dm)),
            rows(dm),
            pl.BlockSpec((1, dm), lambda i: (0, 0)),
            resident((None, dm, nq)),
        ],
        out_specs=[rows(dm), rows(dm), rows(nq)],
        out_shape=[jax.ShapeDtypeStruct((m, dm), F32), jax.ShapeDtypeStruct((m, dm), BF16),
                   jax.ShapeDtypeStruct((m, nq), F32)],
        compiler_params=_cparams("parallel"),
        name="mix_out",
    )(y_main, y_tail, w_out, res, gain.reshape(1, dm), w_q)


def _rmsnorm_kernel(x_ref, g_ref, o_ref):
    x = x_ref[...]
    o_ref[...] = x * _rms_scale(x, x.shape[-1]) * g_ref[...]


def _rmsnorm_rows(x, gain, row_start, n_rows):
    k = x.shape[1]
    tm = _pick_tile(n_rows, 512, CHUNK)
    assert row_start % tm == 0
    first = row_start // tm
    return pl.pallas_call(
        _rmsnorm_kernel,
        grid=(n_rows // tm,),
        in_specs=[pl.BlockSpec((tm, k), lambda i: (first + i, 0)), pl.BlockSpec((1, k), lambda i: (0, 0))],
        out_specs=pl.BlockSpec((tm, k), lambda i: (i, 0)),
        out_shape=jax.ShapeDtypeStruct((n_rows, k), F32),
        compiler_params=_cparams("parallel"),
        name="final_rmsnorm",
    )(x, gain.reshape(1, k))


def _scan_kernel(proj_ref, cw_ref, cb_ref, prm_ref, dsk_ref, gss_ref, gml_ref,
                 ycat_ref, ymeta_ref, ssd_ref, conv_ref, cst_ref, nst_ref, mst_ref,
                 st_scr, tail_scr, cbuf_scr, c_scr, n_scr, m_scr, ybuf_scr, *, d: Dims):
    b = pl.program_id(0)
    c = pl.program_id(1)
    L = CHUNK
    H, P, N, G = d.ssd_heads, d.ssd_p, d.ssd_n, d.ssd_groups
    SW, CC, GP = d.ssd_width, d.conv_ch, d.hpg * d.ssd_p
    MH, DK, DV = d.ml_heads, d.ml_dk, d.ml_dv

    @pl.when(c == 0)
    def _init():
        st_scr[...] = jnp.zeros_like(st_scr)
        tail_scr[...] = jnp.zeros_like(tail_scr)
        c_scr[...] = jnp.zeros_like(c_scr)
        n_scr[...] = jnp.zeros_like(n_scr)
        m_scr[...] = jnp.zeros_like(m_scr)

    rowi = lax.broadcasted_iota(jnp.int32, (L, 1), 0)
    lo = b * d.n_meta
    valid = (c > 0) | ((rowi >= lo) & (rowi < lo + d.n_meta))

    xp = jnp.where(valid, proj_ref[:, d.off_xbc:d.off_xbc + CC], 0.0)
    cbuf_scr[0:SUBLANES, :] = tail_scr[...]
    cbuf_scr[SUBLANES:SUBLANES + L, :] = xp
    acc = cb_ref[...] + cw_ref[3:4, :] * xp
    for kk in range(d.conv_w - 1):
        shift = d.conv_w - 1 - kk
        acc = acc + cw_ref[kk:kk + 1, :] * cbuf_scr[SUBLANES - shift:SUBLANES - shift + L, :]
    xbc = acc * _sigmoid(acc)
    ts = pl.multiple_of(jnp.where(c == 0, lo + d.n_meta, L), SUBLANES)
    tail_scr[...] = cbuf_scr[pl.ds(ts, SUBLANES), :]
    xs = xbc[:, :SW]
    bm = xbc[:, SW:SW + G * N]
    cm = xbc[:, SW + G * N:]

    lane = lax.broadcasted_iota(jnp.int32, (1, LANES), 1)
    is_h = lane < H
    is_m = lane < MH
    dt = jnp.where(valid & is_h, _softplus(proj_ref[:, d.off_dt:d.off_dt + LANES] + prm_ref[0:1, :]), 0.0)
    a_neg = jnp.where(is_h, -jnp.exp(prm_ref[1:2, :]), 0.0)
    li = jnp.where(valid, proj_ref[:, d.off_ig:d.off_ig + LANES] + prm_ref[2:3, :], -jnp.inf)
    lf = jnp.where(valid & is_m, -_softplus(-(proj_ref[:, d.off_fg:d.off_fg + LANES] + prm_ref[3:4, :])), 0.0)
    r2 = lax.broadcasted_iota(jnp.int32, (L, L), 0)
    c2 = lax.broadcasted_iota(jnp.int32, (L, L), 1)
    tril = c2 <= r2
    tri = jnp.where(tril, 1.0, 0.0).astype(F32)
    cums = jnp.dot(tri, jnp.concatenate([dt * a_neg, lf], axis=1), precision=HIGHEST,
                   preferred_element_type=F32)
    cum = cums[:, :LANES]
    bc = cums[:, LANES:]
    cum_t = cum.T
    bc_t = bc.T
    li_t = li.T
    cum_last = cum[L - 1:L, :]

    dt_e, ecum_e, dend_e, elast_e = _expand_many(
        [dt, jnp.where(is_h, jnp.exp(cum), 0.0), jnp.where(is_h, jnp.exp(cum_last - cum), 0.0),
         jnp.broadcast_to(jnp.where(is_h, jnp.exp(cum_last), 0.0), (SUBLANES, LANES))],
        _head_expand(H, P))
    elast_e = elast_e[0:1, :]

    xdt = xs * dt_e
    xdt_b = xdt.astype(BF16)
    xdd_b = (xdt * dend_e).astype(BF16)
    y_parts = []
    for g in range(G):
        cg = cm[:, g * N:(g + 1) * N].astype(BF16)
        bg = bm[:, g * N:(g + 1) * N].astype(BF16)
        cb = lax.dot_general(cg, bg, NT_DIMS, preferred_element_type=F32)
        sg = st_scr[:, g * GP:(g + 1) * GP]
        ys = jnp.dot(cg, sg.astype(BF16), preferred_element_type=F32)
        for hh in range(d.hpg):
            h = g * d.hpg + hh
            seg = cum[:, h:h + 1] - cum_t[h:h + 1, :]
            mat = (cb * jnp.exp(jnp.where(tril, seg, -jnp.inf))).astype(BF16)
            y_parts.append(jnp.dot(mat, xdt_b[:, h * P:(h + 1) * P], preferred_element_type=F32)
                           + ys[:, hh * P:(hh + 1) * P] * ecum_e[:, h * P:(h + 1) * P])
        st_scr[:, g * GP:(g + 1) * GP] = sg * elast_e[:, g * GP:(g + 1) * GP] + lax.dot_general(
            bg, xdd_b[:, g * GP:(g + 1) * GP], TN_DIMS, preferred_element_type=F32)
    y = jnp.concatenate(y_parts, axis=1) + dsk_ref[...] * xs
    z = proj_ref[:, d.off_z:d.off_z + SW]
    yz = y * (z * _sigmoid(z))
    ybuf_scr[:, :SW] = (yz * _rms_scale(yz, SW) * gss_ref[...]).astype(BF16)

    for hh in range(MH):
        qh = proj_ref[:, d.off_q + hh * DK:d.off_q + (hh + 1) * DK]
        kh = proj_ref[:, d.off_k + hh * DK:d.off_k + (hh + 1) * DK] * (DK ** -0.5)
        vh = proj_ref[:, d.off_v + hh * DV:d.off_v + (hh + 1) * DV]
        qb, kb, vb = qh.astype(BF16), kh.astype(BF16), vh.astype(BF16)
        bcol = bc[:, hh:hh + 1]
        brow = bc_t[hh:hh + 1, :]
        m_prev = m_scr[hh:hh + 1, 0:1]
        dm = jnp.where(tril, bcol - brow + li_t[hh:hh + 1, :], -jnp.inf)
        inter = bcol + m_prev
        m_t = jnp.maximum(inter, jnp.max(dm, axis=1, keepdims=True))
        w_inter = jnp.exp(inter - m_t)
        qkw = lax.dot_general(qb, kb, NT_DIMS, preferred_element_type=F32) * jnp.exp(dm - m_t)
        c_prev = c_scr[hh]
        n_prev = n_scr[hh:hh + 1, :]
        num = (jnp.dot(qkw.astype(BF16), vb, preferred_element_type=F32)
               + w_inter * jnp.dot(qb, c_prev.astype(BF16), preferred_element_type=F32))
        den = (jnp.sum(qkw, axis=1, keepdims=True)
               + w_inter * jnp.sum(qh * n_prev, axis=1, keepdims=True))
        hv = num / jnp.maximum(jnp.abs(den), jnp.exp(-m_t))
        b_last = bcol[L - 1:L, :]
        end_inter = b_last + m_prev
        m_new = jnp.maximum(end_inter, jnp.max(b_last - brow + li_t[hh:hh + 1, :], axis=1, keepdims=True))
        we = jnp.exp(b_last - bcol + li[:, hh:hh + 1] - m_new)
        carry = jnp.exp(end_inter - m_new)
        kw = kh * we
        c_scr[hh] = carry * c_prev + lax.dot_general(kw.astype(BF16), vb, TN_DIMS, preferred_element_type=F32)
        n_scr[hh:hh + 1, :] = carry * n_prev + jnp.sum(kw, axis=0, keepdims=True)
        m_scr[hh:hh + 1, :] = jnp.broadcast_to(m_new, (1, LANES))
        og = proj_ref[:, d.off_og + hh * DV:d.off_og + (hh + 1) * DV]
        hm = hv * _rms_scale(hv, DV) * gml_ref[:, hh * DV:(hh + 1) * DV] * _sigmoid(og)
        ybuf_scr[:, SW + hh * DV:SW + (hh + 1) * DV] = hm.astype(BF16)

    @pl.when(c > 0)
    def _():
        ycat_ref[...] = ybuf_scr[...]

    @pl.when(c == 0)
    def _():
        ymeta_ref[0] = ybuf_scr[pl.ds(pl.multiple_of(lo, d.n_meta), d.n_meta), :]

    @pl.when(c == pl.num_programs(1) - 1)
    def _():
        ssd_ref[0] = st_scr[...].T.reshape(H, P, N)
        conv_ref[0] = tail_scr[...]
        cst_ref[0] = c_scr[...]
        nst_ref[0] = n_scr[...]
        mst_ref[0] = m_scr[...]


def _prompt_scan(proj, lw, d: Dims):
    nchunk = d.seq // CHUNK
    meta_blk = (d.rows_main + d.rows_sample) // CHUNK
    width = d.ssd_width + d.ml_width

    def row_map(b, c):
        return (jnp.where(c == 0, meta_blk, b * nchunk + c - 1), 0)

    def out_map(b, c):
        return (b * nchunk + jnp.maximum(c - 1, 0), 0)

    const2 = lambda b, c: (0, 0)
    n_sub = _round_up(d.ml_heads, SUBLANES)
    return pl.pallas_call(
        functools.partial(_scan_kernel, d=d),
        grid=(d.batch, nchunk + 1),
        in_specs=[
            pl.BlockSpec((CHUNK, d.proj_cols), row_map),
            pl.BlockSpec((d.conv_w, d.conv_ch), const2),
            pl.BlockSpec((1, d.conv_ch), const2),
            pl.BlockSpec((SUBLANES, LANES), const2),
            pl.BlockSpec((1, d.ssd_width), const2),
            pl.BlockSpec((1, d.ssd_width), const2),
            pl.BlockSpec((1, d.ml_width), const2),
        ],
        out_specs=[
            pl.BlockSpec((CHUNK, width), out_map),
            pl.BlockSpec((1, d.n_meta, width), lambda b, c: (b, 0, 0)),
            pl.BlockSpec((1, d.ssd_heads, d.ssd_p, d.ssd_n), lambda b, c: (b, 0, 0, 0)),
            pl.BlockSpec((1, SUBLANES, d.conv_ch), lambda b, c: (b, 0, 0)),
            pl.BlockSpec((1, d.ml_heads, d.ml_dk, d.ml_dv), lambda b, c: (b, 0, 0, 0)),
            pl.BlockSpec((1, n_sub, d.ml_dk), lambda b, c: (b, 0, 0)),
            pl.BlockSpec((1, n_sub, LANES), lambda b, c: (b, 0, 0)),
        ],
        out_shape=[
            jax.ShapeDtypeStruct((d.rows_main, width), BF16),
            jax.ShapeDtypeStruct((d.batch, d.n_meta, width), BF16),
            jax.ShapeDtypeStruct((d.batch, d.ssd_heads, d.ssd_p, d.ssd_n), F32),
            jax.ShapeDtypeStruct((d.batch, SUBLANES, d.conv_ch), F32),
            jax.ShapeDtypeStruct((d.batch, d.ml_heads, d.ml_dk, d.ml_dv), F32),
            jax.ShapeDtypeStruct((d.batch, n_sub, d.ml_dk), F32),
            jax.ShapeDtypeStruct((d.batch, n_sub, LANES), F32),
        ],
        scratch_shapes=[
            pltpu.VMEM((d.ssd_n, d.ssd_width), F32),
            pltpu.VMEM((SUBLANES, d.conv_ch), F32),
            pltpu.VMEM((SUBLANES + CHUNK, d.conv_ch), F32),
            pltpu.VMEM((d.ml_heads, d.ml_dk, d.ml_dv), F32),
            pltpu.VMEM((n_sub, d.ml_dk), F32),
            pltpu.VMEM((n_sub, LANES), F32),
            pltpu.VMEM((CHUNK, width), BF16),
        ],
        compiler_params=_cparams("parallel", "arbitrary"),
        name="prompt_scan",
    )(proj, lw["conv_w"], lw["conv_b"], lw["prm"], lw["d_skip_e"], lw["g_ssd"], lw["g_ml"])


SAMPLE_BT = SUBLANES


def _sample_ssd_kernel(proj_ref, hist_ref, sst_ref, cw_ref, cb_ref, prm_ref, dsk_ref, gss_ref, *rest,
                       d: Dims, aliased):
    y_ref, ossd_ref, oconv_ref = rest[1:] if aliased else rest
    H, P, N, G = d.ssd_heads, d.ssd_p, d.ssd_n, d.ssd_groups
    SW, CC, GP = d.ssd_width, d.conv_ch, d.hpg * d.ssd_p
    bt = SAMPLE_BT
    r0 = pl.multiple_of((pl.program_id(0) * bt) % CHUNK, bt)
    xp = proj_ref[pl.ds(r0, bt), d.off_xbc:d.off_xbc + CC]
    acc = cb_ref[...] + cw_ref[3:4, :] * xp
    for kk in range(d.conv_w - 1):
        acc = acc + cw_ref[kk:kk + 1, :] * hist_ref[kk]
    for kk in range(d.conv_w - 2):
        oconv_ref[kk] = hist_ref[kk + 1]
    oconv_ref[d.conv_w - 2] = xp
    xbc = acc * _sigmoid(acc)
    xs = xbc[:, :SW]
    bm = xbc[:, SW:SW + G * N]
    cm = xbc[:, SW + G * N:]

    lane = lax.broadcasted_iota(jnp.int32, (1, LANES), 1)
    is_h = lane < H
    dt = jnp.where(is_h, _softplus(proj_ref[pl.ds(r0, bt), d.off_dt:d.off_dt + LANES] + prm_ref[0:1, :]), 0.0)
    dec = jnp.where(is_h, jnp.exp(dt * -jnp.exp(prm_ref[1:2, :])), 0.0)
    e_ssd = _head_expand(H, P)
    xdt_t = _pad_rows(xs * _expand(dt, e_ssd), LANES).T
    dec_t = _pad_rows(_expand(dec, e_ssd), LANES).T
    lane_full = lax.broadcasted_iota(jnp.int32, (SW, LANES), 1)
    y_t = jnp.zeros((SW, LANES), F32)
    for i in range(bt):
        st = sst_ref[i].reshape(SW, N)
        b_full = jnp.concatenate(
            [jnp.broadcast_to(bm[i:i + 1, g * N:(g + 1) * N], (GP, N)) for g in range(G)], axis=0)
        c_full = jnp.concatenate(
            [jnp.broadcast_to(cm[i:i + 1, g * N:(g + 1) * N], (GP, N)) for g in range(G)], axis=0)
        new = st * dec_t[:, i:i + 1] + xdt_t[:, i:i + 1] * b_full
        ossd_ref[i] = new.reshape(H, P, N)
        y_t = jnp.where(lane_full == i, jnp.sum(new * c_full, axis=1, keepdims=True), y_t)
    y = y_t.T[:bt, :] + dsk_ref[...] * xs
    z = proj_ref[pl.ds(r0, bt), d.off_z:d.off_z + SW]
    yz = y * (z * _sigmoid(z))
    y_ref[...] = (yz * _rms_scale(yz, SW) * gss_ref[...]).astype(y_ref.dtype)


def _sample_ssd(proj, hist, sst_all, prev_ssd, l, lw, d: Dims):
    bt = SAMPLE_BT
    base_blk = d.rows_main // CHUNK
    const2 = lambda i: (0, 0)
    state_blk = pl.BlockSpec((None, bt, d.ssd_heads, d.ssd_p, d.ssd_n), lambda i: (l, i, 0, 0, 0))
    in_specs = [
        pl.BlockSpec((CHUNK, d.proj_cols), lambda i: (base_blk + (i * bt) // CHUNK, 0)),
        pl.BlockSpec((d.conv_w - 1, bt, d.conv_ch), lambda i: (0, i, 0)),
        state_blk,
        pl.BlockSpec((d.conv_w, d.conv_ch), const2),
        pl.BlockSpec((1, d.conv_ch), const2),
        pl.BlockSpec((SUBLANES, LANES), const2),
        pl.BlockSpec((1, d.ssd_width), const2),
        pl.BlockSpec((1, d.ssd_width), const2),
    ]
    args = [proj, hist, sst_all, lw["conv_w"], lw["conv_b"], lw["prm"], lw["d_skip_e"], lw["g_ssd"]]
    aliases = {}
    if prev_ssd is not None:
        in_specs.append(pl.BlockSpec(memory_space=pl.ANY))
        args.append(prev_ssd)
        aliases = {len(args) - 1: 1}
    return pl.pallas_call(
        functools.partial(_sample_ssd_kernel, d=d, aliased=prev_ssd is not None),
        grid=(d.dec_batch // bt,),
        in_specs=in_specs,
        out_specs=[
            pl.BlockSpec((bt, d.ssd_width), lambda i: (i, 0)),
            state_blk,
            pl.BlockSpec((d.conv_w - 1, bt, d.conv_ch), lambda i: (0, i, 0)),
        ],
        out_shape=[
            jax.ShapeDtypeStruct((d.dec_batch, d.ssd_width), F32),
            jax.ShapeDtypeStruct(sst_all.shape, F32),
            jax.ShapeDtypeStruct((d.conv_w - 1, d.dec_batch, d.conv_ch), F32),
        ],
        input_output_aliases=aliases,
        compiler_params=_cparams("parallel"),
        name="sample_ssd",
    )(*args)


def _sample_mlstm_kernel(proj_ref, cst_ref, nst_ref, mst_ref, prm_ref, gml_ref, *rest, d: Dims, aliased):
    y_ref, oc_ref, on_ref, om_ref = rest[1:] if aliased else rest
    MH, DK, DV = d.ml_heads, d.ml_dk, d.ml_dv
    QK, MW = d.ml_qk, d.ml_width
    bt = SAMPLE_BT
    r0 = pl.multiple_of((pl.program_id(0) * bt) % CHUNK, bt)
    rows = pl.ds(r0, bt)
    lane = lax.broadcasted_iota(jnp.int32, (1, LANES), 1)
    is_m = lane < MH
    li = proj_ref[rows, d.off_ig:d.off_ig + LANES] + prm_ref[2:3, :]
    lf = -_softplus(-(proj_ref[rows, d.off_fg:d.off_fg + LANES] + prm_ref[3:4, :]))
    inter = lf + mst_ref[...]
    m_t = jnp.maximum(inter, li)
    w_intra = jnp.where(is_m, jnp.exp(li - m_t), 0.0)
    w_inter = jnp.where(is_m, jnp.exp(inter - m_t), 0.0)
    om_ref[...] = m_t

    e_qk = _head_expand(MH, DK)
    e_v = _head_expand(MH, DV)
    q = proj_ref[rows, d.off_q:d.off_q + QK]
    k = proj_ref[rows, d.off_k:d.off_k + QK] * (DK ** -0.5)
    v = proj_ref[rows, d.off_v:d.off_v + MW]
    wi_e = _expand(w_intra, e_qk)
    wn_e = _expand(w_inter, e_qk)
    kw = k * wi_e
    n_new = wn_e * nst_ref[...] + kw
    on_ref[...] = n_new
    den = lax.dot_general(q * n_new, e_qk, NT_DIMS, precision=HIGHEST, preferred_element_type=F32)
    den = jnp.maximum(jnp.abs(den), jnp.exp(-m_t))
    kw_t = _pad_rows(kw, LANES).T
    q_t = _pad_rows(q, LANES).T
    wn_t = _pad_rows(wn_e, LANES).T
    num_rows = []
    for i in range(bt):
        parts = []
        for hh in range(MH):
            rs = slice(hh * DK, (hh + 1) * DK)
            c_new = cst_ref[i, hh] * wn_t[rs, i:i + 1] + kw_t[rs, i:i + 1] * v[i:i + 1, hh * DV:(hh + 1) * DV]
            oc_ref[i, hh] = c_new
            parts.append(jnp.sum(c_new * q_t[rs, i:i + 1], axis=0, keepdims=True))
        num_rows.append(jnp.concatenate(parts, axis=1))
    hv = jnp.concatenate(num_rows, axis=0) / _expand(den, e_v)
    ms = lax.dot_general(hv * hv, e_v, NT_DIMS, precision=HIGHEST, preferred_element_type=F32) * (1.0 / DV)
    og = proj_ref[rows, d.off_og:d.off_og + MW]
    y_ref[...] = (hv * _expand(lax.rsqrt(ms + RMS_EPS), e_v) * gml_ref[...] * _sigmoid(og)).astype(y_ref.dtype)


def _sample_mlstm(proj, cst_all, nst, mst, prev_c, l, lw, d: Dims):
    bt = SAMPLE_BT
    base_blk = d.rows_main // CHUNK
    const2 = lambda i: (0, 0)
    state_blk = pl.BlockSpec((None, bt, d.ml_heads, d.ml_dk, d.ml_dv), lambda i: (l, i, 0, 0, 0))
    in_specs = [
        pl.BlockSpec((CHUNK, d.proj_cols), lambda i: (base_blk + (i * bt) // CHUNK, 0)),
        state_blk,
        pl.BlockSpec((bt, d.ml_qk), lambda i: (i, 0)),
        pl.BlockSpec((bt, LANES), lambda i: (i, 0)),
        pl.BlockSpec((SUBLANES, LANES), const2),
        pl.BlockSpec((1, d.ml_width), const2),
    ]
    args = [proj, cst_all, nst, mst, lw["prm"], lw["g_ml"]]
    aliases = {}
    if prev_c is not None:
        in_specs.append(pl.BlockSpec(memory_space=pl.ANY))
        args.append(prev_c)
        aliases = {len(args) - 1: 1}
    return pl.pallas_call(
        functools.partial(_sample_mlstm_kernel, d=d, aliased=prev_c is not None),
        grid=(d.dec_batch // bt,),
        in_specs=in_specs,
        out_specs=[
            pl.BlockSpec((bt, d.ml_width), lambda i: (i, 0)),
            state_blk,
            pl.BlockSpec((bt, d.ml_qk), lambda i: (i, 0)),
            pl.BlockSpec((bt, LANES), lambda i: (i, 0)),
        ],
        out_shape=[
            jax.ShapeDtypeStruct((d.dec_batch, d.ml_width), F32),
            jax.ShapeDtypeStruct(cst_all.shape, F32),
            jax.ShapeDtypeStruct((d.dec_batch, d.ml_qk), F32),
            jax.ShapeDtypeStruct((d.dec_batch, LANES), F32),
        ],
        input_output_aliases=aliases,
        compiler_params=_cparams("parallel"),
        name="sample_mlstm",
    )(*args)


NO_ID = 1e9


def _topk_rows(s, ids, k):
    vals, sel = [], []
    for _ in range(k):
        m = jnp.max(s, axis=0, keepdims=True)
        idx = jnp.min(jnp.where(s == m, ids, NO_ID), axis=0, keepdims=True)
        vals.append(m)
        sel.append(idx)
        s = jnp.where(ids == idx, -jnp.inf, s)
    return jnp.concatenate(vals, axis=0), jnp.concatenate(sel, axis=0)


def _select_rows(sel, table, k):
    out = jnp.zeros(sel.shape, table.dtype)
    for a in range(k):
        out = jnp.where(sel == a, table[a:a + 1, :], out)
    return out


def _pair_candidates(v1, v2, k):
    tt = v1.shape[1]
    vals, ids = [], []

    def piece(values, flat_ids, lo, hi):
        row = lax.broadcasted_iota(jnp.int32, values.shape, 0)
        ok = (row >= lo) & (row < hi)
        vals.append(jnp.where(ok, values, -jnp.inf))
        ids.append(jnp.where(ok, flat_ids, NO_ID))

    def iota_f(rows):
        return lax.broadcasted_iota(jnp.int32, (rows, tt), 0).astype(F32)

    split = 0
    while (split + 1) * (split + 1) <= k:
        split += 1
    for a in range(split):
        nb = k // (a + 1)
        rows = _round_up(nb, SUBLANES)
        piece(v1[a:a + 1, :] + v2[:rows, :], a * k + iota_f(rows), 0, nb)
    for b in range(k):
        hi = k // (b + 1)
        if hi <= split:
            break
        rows = _round_up(hi, SUBLANES)
        piece(v1[:rows, :] + v2[b:b + 1, :], iota_f(rows) * k + b, split, hi)
    return jnp.concatenate(vals, axis=0), jnp.concatenate(ids, axis=0)


def _route_kernel(q_ref, sk_ref, r_ref, c_ref, g_ref, *, heads, k):
    tt = q_ref.shape[0]
    keys, half = sk_ref.shape[2], sk_ref.shape[3]
    key_ids = lax.broadcasted_iota(jnp.int32, (keys, tt), 0).astype(F32)
    r_parts, c_parts, g_parts = [], [], []
    for h in range(heads):
        tops = []
        for side in range(2):
            col = (2 * h + side) * half
            s = lax.dot_general(sk_ref[side, h], q_ref[:, col:col + half], NT_DIMS,
                                precision=HIGHEST, preferred_element_type=F32)
            tops.append(_topk_rows(s, key_ids, k))
        (v1, i1), (v2, i2) = tops
        sc, ci = _topk_rows(*_pair_candidates(v1, v2, k), k)
        a_sel = jnp.floor(ci * (1.0 / k))
        r_parts.append(_select_rows(a_sel, i1, k))
        c_parts.append(_select_rows(ci - a_sel * k, i2, k))
        e = jnp.exp(sc - sc[0:1, :])
        g_parts.append(e / jnp.sum(e, axis=0, keepdims=True))
    pad = LANES - heads * k
    if pad:
        r_parts.append(jnp.zeros((pad, tt), F32))
        c_parts.append(jnp.zeros((pad, tt), F32))
        g_parts.append(jnp.zeros((pad, tt), F32))
    r_ref[...] = jnp.concatenate(r_parts, axis=0).T.astype(jnp.int32)
    c_ref[...] = jnp.concatenate(c_parts, axis=0).T.astype(jnp.int32)
    g_ref[...] = jnp.concatenate(g_parts, axis=0).T


def _peer_route(q, sub_keys, d: Dims):
    m = q.shape[0]
    tt = LANES
    spec = pl.BlockSpec((tt, LANES), lambda i: (i, 0))
    return pl.pallas_call(
        functools.partial(_route_kernel, heads=d.peer_heads, k=PEER_TOPK),
        grid=(m // tt,),
        in_specs=[
            pl.BlockSpec((tt, q.shape[1]), lambda i: (i, 0)),
            pl.BlockSpec(sub_keys.shape, lambda i: (0, 0, 0, 0)),
        ],
        out_specs=[spec, spec, spec],
        out_shape=[jax.ShapeDtypeStruct((m, LANES), jnp.int32), jax.ShapeDtypeStruct((m, LANES), jnp.int32),
                   jax.ShapeDtypeStruct((m, LANES), F32)],
        compiler_params=_cparams("parallel"),
        name="peer_route",
    )(q, sub_keys)


GATE_TT = 128
PAIR = 2 * SUBLANES
HIGH_HALF = 0xFFFF0000


def _gate_kernel(r_ref, c_ref, g_ref, w_ref):
    tt = r_ref.shape[0]
    sub = lax.broadcasted_iota(jnp.int32, (LANES, LANES), 0)
    zero = jnp.zeros((LANES, LANES), BF16)

    def gated_rows(row):
        on_r = jnp.broadcast_to(r_ref[row, :], (LANES, LANES)) == sub
        return jnp.where(on_r, jnp.broadcast_to(g_ref[row, :], (LANES, LANES)), 0.0).astype(BF16)

    def onehot_cols(row):
        return jnp.where(jnp.broadcast_to(c_ref[row, :], (LANES, LANES)) == sub, 1.0, 0.0).astype(BF16)

    def bf16_bits(x):
        return lax.bitcast_convert_type(x.astype(BF16).astype(F32), jnp.uint32)

    def per_pair(pg, carry):
        base = pl.multiple_of(pg * PAIR, PAIR)
        for tl in range(SUBLANES):
            ra = pl.ds(base + tl, 1)
            rb = pl.ds(base + SUBLANES + tl, 1)
            lhs = jnp.concatenate([gated_rows(ra), gated_rows(rb)], axis=1)
            rhs = jnp.concatenate([jnp.concatenate([onehot_cols(ra), zero], axis=1),
                                   jnp.concatenate([zero, onehot_cols(rb)], axis=1)], axis=0)
            w2 = lax.dot_general(lhs, rhs, NT_DIMS, preferred_element_type=F32)
            word = (lax.shift_right_logical(bf16_bits(w2[:, :LANES]), jnp.uint32(16))
                    | (bf16_bits(w2[:, LANES:]) & jnp.uint32(HIGH_HALF)))
            w_ref[pg, pl.ds(tl, LANES, stride=SUBLANES), :] = word
        return carry

    lax.fori_loop(0, tt // PAIR, per_pair, 0, unroll=4)


def _peer_gates(r_idx, c_idx, gate):
    m = r_idx.shape[0]
    tt = GATE_TT
    spec = pl.BlockSpec((tt, LANES), lambda i: (i, 0))
    return pl.pallas_call(
        _gate_kernel,
        grid=(m // tt,),
        in_specs=[spec, spec, spec],
        out_specs=pl.BlockSpec((tt // PAIR, LANES * SUBLANES, LANES), lambda i: (i, 0, 0)),
        out_shape=jax.ShapeDtypeStruct((m // PAIR, LANES * SUBLANES, LANES), jnp.uint32),
        compiler_params=_cparams("parallel"),
        name="peer_gates",
    )(r_idx, c_idx, gate)


def _gelu(x):
    return 0.5 * x * (1.0 + lax.erf(x * math.sqrt(0.5)))


def _peer_kernel(x_ref, u_ref, v_ref, w_ref, h_ref, o_ref):
    tt = x_ref.shape[0]
    ce = u_ref.shape[0]

    @pl.when(pl.program_id(1) == 0)
    def _():
        o_ref[...] = h_ref[...]

    a = lax.dot_general(x_ref[...], u_ref[...], NT_DIMS, preferred_element_type=F32)

    def gates_of(r):
        blk = w_ref[:, r * SUBLANES:(r + 1) * SUBLANES, :]
        even = lax.bitcast_convert_type(lax.shift_left(blk, jnp.uint32(16)), F32)
        odd = lax.bitcast_convert_type(blk & jnp.uint32(HIGH_HALF), F32)
        return jnp.stack([even, odd], axis=1).reshape(tt, LANES)

    w = jnp.concatenate([gates_of(r) for r in range(ce // LANES)], axis=1)
    o_ref[...] += jnp.dot((_gelu(a) * w).astype(BF16), v_ref[...], preferred_element_type=F32)


def _peer_experts(x, u, v, w, h, l):
    m, k = x.shape
    e = u.shape[1]
    tt = _pick_tile(m, 768, CHUNK)
    ce = _pick_tile(e, 512, LANES)
    rows_spec = pl.BlockSpec((tt, k), lambda i, j: (i, 0))
    tab_spec = pl.BlockSpec((None, ce, k), lambda i, j: (l, j, 0))
    return pl.pallas_call(
        _peer_kernel,
        grid=(m // tt, e // ce),
        in_specs=[
            rows_spec, tab_spec, tab_spec,
            pl.BlockSpec((tt // PAIR, ce // LANES * SUBLANES, LANES), lambda i, j: (i, j, 0)),
            pl.BlockSpec((tt, k), lambda i, j: (i, 0), pipeline_mode=pl.Buffered(1)),
        ],
        out_specs=rows_spec,
        out_shape=jax.ShapeDtypeStruct((m, k), F32),
        compiler_params=_cparams("parallel", "arbitrary"),
        name="peer_experts",
    )(x, u, v, w, h)


def _lane_row(x):
    return jnp.pad(x.astype(F32), (0, LANES - x.shape[0])).reshape(1, LANES)


def _pack_w_in(w_in, d: Dims):
    sizes = (d.ssd_width, d.conv_ch, d.ssd_heads, d.ml_qk, d.ml_qk, d.ml_width, d.ml_heads, d.ml_heads, d.ml_width)
    offs = [0]
    for s in sizes:
        offs.append(offs[-1] + s)
    col = lambda i: w_in[:, :, offs[i]:offs[i + 1]]
    pad_lanes = lambda w: jnp.pad(w, ((0, 0), (0, 0), (0, LANES - w.shape[2])))
    w_z, w_xbc, w_dt, w_q, w_k, w_v, w_ig, w_fg, w_og = [col(i) for i in range(9)]
    return jnp.concatenate(
        [w_xbc, w_q, w_k, w_z, w_og, w_v, pad_lanes(w_dt), pad_lanes(w_ig), pad_lanes(w_fg)], axis=2).astype(BF16)


def _pack_layer(l, d: Dims, conv_w, conv_b, dt_bias, a_log, d_skip, g_ssd_out, i_bias, f_bias, g_mlstm_out):
    prm = jnp.concatenate(
        [_lane_row(dt_bias[l]), _lane_row(a_log[l]), _lane_row(i_bias[l]), _lane_row(f_bias[l]),
         jnp.zeros((SUBLANES - 4, LANES), F32)], axis=0)
    return dict(
        conv_w=conv_w[l], conv_b=conv_b[l].reshape(1, -1), prm=prm,
        d_skip_e=jnp.repeat(d_skip[l], d.ssd_p).reshape(1, -1),
        g_ssd=g_ssd_out[l].reshape(1, -1), g_ml=g_mlstm_out[l].reshape(1, -1),
    )


def _derive_dims(x_prompt, x_sample, state_ssd, state_conv, state_mlstm_c, meta_tokens, peer_sub_keys, peer_u):
    batch, seq, d_model = x_prompt.shape
    _, dec_batch, heads, p, n = state_ssd.shape
    conv_ch = state_conv.shape[-1]
    groups = (conv_ch - heads * p) // (2 * n)
    _, _, mh, dk, dv = state_mlstm_c.shape
    _, ph, keys, half = peer_sub_keys.shape[1:]
    d = Dims(d_model=d_model, batch=batch, seq=seq, dec_batch=dec_batch, n_meta=meta_tokens.shape[0],
             ssd_heads=heads, ssd_p=p, ssd_n=n, ssd_groups=groups, conv_w=state_conv.shape[2] + 1,
             ml_heads=mh, ml_dk=dk, ml_dv=dv, peer_heads=ph, peer_keys=keys, peer_half=half,
             n_experts=peer_u.shape[1])
    assert x_sample.shape[1] == 1 and seq % CHUNK == 0
    assert d.batch * d.n_meta <= CHUNK and d.n_meta % 16 == 0
    assert d.dec_batch % SAMPLE_BT == 0
    assert d.ssd_n == LANES and d.peer_keys == LANES and d.peer_half == LANES
    assert d.ssd_heads <= LANES and d.ml_heads <= SUBLANES and d.conv_w == 4
    assert d.peer_heads * PEER_TOPK <= LANES and d.n_experts == d.peer_keys ** 2
    assert PEER_TOPK & (PEER_TOPK - 1) == 0
    assert d.rows_main % MIX_TM == 0 and (d.rows_sample + d.rows_meta) % MIX_TM == 0
    assert all(w % LANES == 0 for w in (d.ssd_width, d.ml_qk, d.ml_width, d.hpg * d.ssd_p, d.ml_dk))
    return d


def kernel(x_prompt, x_sample, state_ssd, state_conv, state_mlstm_c, state_mlstm_n, state_mlstm_m, meta_tokens, g_mix, w_in, conv_w, conv_b, dt_bias, a_log, d_skip, g_ssd_out, mlstm_i_bias, mlstm_f_bias, g_mlstm_out, w_out, g_ffn, peer_w_q, peer_sub_keys, peer_u, peer_v, g_final):
    d = _derive_dims(x_prompt, x_sample, state_ssd, state_conv, state_mlstm_c, meta_tokens, peer_sub_keys, peer_u)
    depth = w_in.shape[0]
    dm = d.d_model
    meta = jnp.broadcast_to(meta_tokens[None].astype(F32), (d.batch, d.n_meta, dm)).reshape(-1, dm)
    h = jnp.concatenate([
        x_prompt.reshape(-1, dm),
        _pad_rows(x_sample.reshape(-1, dm), d.rows_sample),
        _pad_rows(meta, d.rows_meta),
    ], axis=0)

    w_in_b = _pack_w_in(w_in, d)
    w_out_b, w_q_b = w_out.astype(BF16), peer_w_q.astype(BF16)
    u_b, v_b = peer_u.astype(BF16), peer_v.astype(BF16)

    new_p = ([], [], [], [], [])
    new_s = ([], [], [])
    s_ssd = s_c = None
    for l in range(depth):
        lw = _pack_layer(l, d, conv_w, conv_b, dt_bias, a_log, d_skip, g_ssd_out, mlstm_i_bias,
                         mlstm_f_bias, g_mlstm_out)
        proj, _ = _norm_matmul(h, g_mix[l], w_in_b, l)

        y_main, y_meta, p_ssd, p_conv, p_c, p_n, p_m = _prompt_scan(proj, lw, d)
        s_yssd, s_ssd, s_conv = _sample_ssd(
            proj, jnp.transpose(state_conv[l], (1, 0, 2)), state_ssd, s_ssd, l, lw, d)
        m_pad = jnp.pad(state_mlstm_m[l], ((0, 0), (0, LANES - d.ml_heads)))
        s_yml, s_c, s_n, s_m = _sample_mlstm(
            proj, state_mlstm_c, state_mlstm_n[l].reshape(d.dec_batch, d.ml_qk), m_pad, s_c, l, lw, d)

        y_tail = jnp.concatenate([
            _pad_rows(jnp.concatenate([s_yssd, s_yml], axis=1).astype(BF16), d.rows_sample),
            _pad_rows(y_meta.reshape(d.batch * d.n_meta, -1), d.rows_meta),
        ], axis=0)
        h, xn, q = _mix_out(y_main, y_tail, w_out_b, h, g_ffn[l], w_q_b, l)
        r_idx, c_idx, gate = _peer_route(q, peer_sub_keys[l], d)
        h = _peer_experts(xn, u_b, v_b, _peer_gates(r_idx, c_idx, gate), h, l)

        for lst, s in zip(new_p, (p_ssd, p_conv[:, SUBLANES - (d.conv_w - 1):, :], p_c,
                                  p_n[:, :d.ml_heads, :], p_m[:, :d.ml_heads, 0])):
            lst.append(s)
        for lst, s in zip(new_s, (jnp.transpose(s_conv, (1, 0, 2)),
                                  s_n.reshape(d.dec_batch, d.ml_heads, d.ml_dk), s_m[:, :d.ml_heads])):
            lst.append(s)

    y_prompt = _rmsnorm_rows(h, g_final, 0, d.rows_main).reshape(d.batch, d.seq, dm)
    y_sample = _rmsnorm_rows(h, g_final, d.rows_main, d.rows_sample)[:d.dec_batch].reshape(d.dec_batch, 1, dm)
    s_conv_all, s_n_all, s_m_all = (jnp.stack(s) for s in new_s)
    return ((y_prompt, y_sample) + tuple(jnp.stack(s) for s in new_p)
            + (s_ssd, s_conv_all, s_c, s_n_all, s_m_all))
```

```python
import functools
import math
from typing import NamedTuple

import jax
import jax.numpy as jnp
from jax import lax
from jax.experimental import pallas as pl
from jax.experimental.pallas import tpu as pltpu

F32 = jnp.float32
BF16 = jnp.bfloat16
HIGHEST = lax.Precision.HIGHEST

LANES = 128
SUBLANES = 8
V7X_VMEM_LIMIT_BYTES = 56 * 1024 * 1024

CHUNK = 128
PEER_TOPK = 16
RMS_EPS = 1e-6
NT_DIMS = (((1,), (1,)), ((), ()))
TN_DIMS = (((0,), (0,)), ((), ()))


class Dims(NamedTuple):
    d_model: int
    batch: int
    seq: int
    dec_batch: int
    n_meta: int
    ssd_heads: int
    ssd_p: int
    ssd_n: int
    ssd_groups: int
    conv_w: int
    ml_heads: int
    ml_dk: int
    ml_dv: int
    peer_heads: int
    peer_keys: int
    peer_half: int
    n_experts: int

    @property
    def ssd_width(self):
        return self.ssd_heads * self.ssd_p

    @property
    def conv_ch(self):
        return self.ssd_width + 2 * self.ssd_groups * self.ssd_n

    @property
    def ml_width(self):
        return self.ml_heads * self.ml_dv

    @property
    def ml_qk(self):
        return self.ml_heads * self.ml_dk

    @property
    def hpg(self):
        return self.ssd_heads // self.ssd_groups

    @property
    def rows_main(self):
        return self.batch * self.seq

    @property
    def rows_sample(self):
        return _round_up(self.dec_batch, CHUNK)

    @property
    def rows_meta(self):
        return CHUNK

    @property
    def rows(self):
        return self.rows_main + self.rows_sample + self.rows_meta

    @property
    def off_xbc(self):
        return 0

    @property
    def off_q(self):
        return self.conv_ch

    @property
    def off_k(self):
        return self.off_q + self.ml_qk

    @property
    def off_z(self):
        return self.off_k + self.ml_qk

    @property
    def off_og(self):
        return self.off_z + self.ssd_width

    @property
    def off_v(self):
        return self.off_og + self.ml_width

    @property
    def off_dt(self):
        return self.off_v + self.ml_width

    @property
    def off_ig(self):
        return self.off_dt + LANES

    @property
    def off_fg(self):
        return self.off_ig + LANES

    @property
    def proj_cols(self):
        return self.off_fg + LANES


def _round_up(x, m):
    return (x + m - 1) // m * m


def _pick_tile(n, target, quantum):
    best = None
    t = quantum
    while t <= min(n, target):
        if n % t == 0:
            best = t
        t += quantum
    return best if best is not None else n


def _cparams(*sem):
    return pltpu.CompilerParams(dimension_semantics=sem, vmem_limit_bytes=V7X_VMEM_LIMIT_BYTES)


def _sigmoid(x):
    return 1.0 / (1.0 + jnp.exp(-x))


def _softplus(x):
    return jnp.maximum(x, 0.0) + jnp.log1p(jnp.exp(-jnp.abs(x)))


def _rms_scale(x, width):
    return lax.rsqrt(jnp.sum(x * x, axis=-1, keepdims=True) * (1.0 / width) + RMS_EPS)


def _head_expand(n_heads, per_head, total_rows=LANES):
    row = lax.broadcasted_iota(jnp.int32, (total_rows, n_heads * per_head), 0)
    lane = lax.broadcasted_iota(jnp.int32, (total_rows, n_heads * per_head), 1)
    return jnp.where((lane >= row * per_head) & (lane < (row + 1) * per_head), 1.0, 0.0).astype(F32)


def _expand_many(xs, onehot):
    x = jnp.concatenate(xs, axis=0) if len(xs) > 1 else xs[0]
    p1 = x.astype(BF16)
    r1 = x - p1.astype(F32)
    p2 = r1.astype(BF16)
    p3 = (r1 - p2.astype(F32)).astype(BF16)
    e = onehot.astype(BF16)
    out = jnp.dot(jnp.concatenate([p1, p2, p3], axis=1), jnp.concatenate([e, e, e], axis=0),
                  preferred_element_type=F32)
    outs, start = [], 0
    for xi in xs:
        outs.append(out[start:start + xi.shape[0], :])
        start += xi.shape[0]
    return outs


def _expand(x, onehot):
    return _expand_many([x], onehot)[0]


def _pad_rows(x, rows):
    if x.shape[0] == rows:
        return x
    return jnp.concatenate([x, jnp.zeros((rows - x.shape[0], x.shape[1]), x.dtype)], axis=0)


def _norm_matmul_kernel(x_ref, g_ref, w_ref, o_ref, hn_ref):
    @pl.when(pl.program_id(1) == 0)
    def _():
        x = x_ref[...]
        hn_ref[...] = (x * _rms_scale(x, x.shape[-1]) * g_ref[...]).astype(hn_ref.dtype)

    o_ref[...] = lax.dot_general(hn_ref[...], w_ref[...], NT_DIMS, preferred_element_type=F32)


def _norm_matmul(x, gain, w_t, l):
    m, k = x.shape
    n = w_t.shape[1]
    tm = _pick_tile(m, 768, CHUNK)
    tn = _pick_tile(n, 1792, LANES)
    return pl.pallas_call(
        _norm_matmul_kernel,
        grid=(m // tm, n // tn),
        in_specs=[
            pl.BlockSpec((tm, k), lambda i, j: (i, 0)),
            pl.BlockSpec((1, k), lambda i, j: (0, 0)),
            pl.BlockSpec((None, tn, k), lambda i, j: (l, j, 0)),
        ],
        out_specs=[
            pl.BlockSpec((tm, tn), lambda i, j: (i, j)),
            pl.BlockSpec((tm, k), lambda i, j: (i, 0)),
        ],
        out_shape=[jax.ShapeDtypeStruct((m, n), F32), jax.ShapeDtypeStruct((m, k), BF16)],
        compiler_params=_cparams("parallel", "arbitrary"),
        name="norm_matmul",
    )(x, gain.reshape(1, k), w_t)


def _rmsnorm_kernel(x_ref, g_ref, o_ref):
    x = x_ref[...]
    o_ref[...] = x * _rms_scale(x, x.shape[-1]) * g_ref[...]


def _rmsnorm_rows(x, gain, row_start, n_rows):
    k = x.shape[1]
    tm = _pick_tile(n_rows, 512, CHUNK)
    assert row_start % tm == 0
    first = row_start // tm
    return pl.pallas_call(
        _rmsnorm_kernel,
        grid=(n_rows // tm,),
        in_specs=[pl.BlockSpec((tm, k), lambda i: (first + i, 0)), pl.BlockSpec((1, k), lambda i: (0, 0))],
        out_specs=pl.BlockSpec((tm, k), lambda i: (i, 0)),
        out_shape=jax.ShapeDtypeStruct((n_rows, k), F32),
        compiler_params=_cparams("parallel"),
        name="final_rmsnorm",
    )(x, gain.reshape(1, k))


def _scan_kernel(proj_ref, cw_ref, cb_ref, prm_ref, dsk_ref, gss_ref, gml_ref,
                 ycat_ref, ymeta_ref, ssd_ref, conv_ref, cst_ref, nst_ref, mst_ref,
                 st_scr, tail_scr, cbuf_scr, c_scr, n_scr, m_scr, ybuf_scr, *, d: Dims):
    b = pl.program_id(0)
    c = pl.program_id(1)
    L = CHUNK
    H, P, N, G = d.ssd_heads, d.ssd_p, d.ssd_n, d.ssd_groups
    SW, CC, GP = d.ssd_width, d.conv_ch, d.hpg * d.ssd_p
    MH, DK, DV = d.ml_heads, d.ml_dk, d.ml_dv

    @pl.when(c == 0)
    def _init():
        st_scr[...] = jnp.zeros_like(st_scr)
        tail_scr[...] = jnp.zeros_like(tail_scr)
        c_scr[...] = jnp.zeros_like(c_scr)
        n_scr[...] = jnp.zeros_like(n_scr)
        m_scr[...] = jnp.zeros_like(m_scr)

    rowi = lax.broadcasted_iota(jnp.int32, (L, 1), 0)
    lo = b * d.n_meta
    valid = (c > 0) | ((rowi >= lo) & (rowi < lo + d.n_meta))

    xp = jnp.where(valid, proj_ref[:, d.off_xbc:d.off_xbc + CC], 0.0)
    cbuf_scr[...] = xp
    tail = tail_scr[...]
    sub8 = lax.broadcasted_iota(jnp.int32, (SUBLANES, 1), 0)
    acc = cb_ref[...] + cw_ref[d.conv_w - 1:d.conv_w, :] * xp
    for kk in range(d.conv_w - 1):
        shift = d.conv_w - 1 - kk
        rolled = pltpu.roll(xp, shift, axis=0)
        head = jnp.where(sub8 < shift, pltpu.roll(tail, shift, axis=0), rolled[:SUBLANES, :])
        acc = acc + cw_ref[kk:kk + 1, :] * jnp.concatenate([head, rolled[SUBLANES:, :]], axis=0)
    xbc = acc * _sigmoid(acc)
    ts = pl.multiple_of(jnp.where(c == 0, lo + d.n_meta - SUBLANES, L - SUBLANES), SUBLANES)
    tail_scr[...] = cbuf_scr[pl.ds(ts, SUBLANES), :]
    xs = xbc[:, :SW]
    bm = xbc[:, SW:SW + G * N]
    cm = xbc[:, SW + G * N:]

    lane = lax.broadcasted_iota(jnp.int32, (1, LANES), 1)
    is_h = lane < H
    is_m = lane < MH
    dt = jnp.where(valid & is_h, _softplus(proj_ref[:, d.off_dt:d.off_dt + LANES] + prm_ref[0:1, :]), 0.0)
    a_neg = jnp.where(is_h, -jnp.exp(prm_ref[1:2, :]), 0.0)
    li = jnp.where(valid, proj_ref[:, d.off_ig:d.off_ig + LANES] + prm_ref[2:3, :], -jnp.inf)
    lf = jnp.where(valid & is_m, -_softplus(-(proj_ref[:, d.off_fg:d.off_fg + LANES] + prm_ref[3:4, :])), 0.0)
    r2 = lax.broadcasted_iota(jnp.int32, (L, L), 0)
    c2 = lax.broadcasted_iota(jnp.int32, (L, L), 1)
    tril = c2 <= r2
    tri = jnp.where(tril, 1.0, 0.0).astype(F32)
    cums = jnp.dot(tri, jnp.concatenate([dt * a_neg, lf], axis=1), precision=HIGHEST,
                   preferred_element_type=F32)
    cum = cums[:, :LANES]
    bc = cums[:, LANES:]
    cum_t = cum.T
    bc_t = bc.T
    li_t = li.T
    cum_last = cum[L - 1:L, :]

    dt_e, ecum_e, dend_e, elast_e = _expand_many(
        [dt, jnp.where(is_h, jnp.exp(cum), 0.0), jnp.where(is_h, jnp.exp(cum_last - cum), 0.0),
         jnp.broadcast_to(jnp.where(is_h, jnp.exp(cum_last), 0.0), (SUBLANES, LANES))],
        _head_expand(H, P))
    elast_e = elast_e[0:1, :]

    xdt = xs * dt_e
    xdt_b = xdt.astype(BF16)
    xdd_b = (xdt * dend_e).astype(BF16)
    y_parts = []
    for g in range(G):
        cg = cm[:, g * N:(g + 1) * N].astype(BF16)
        bg = bm[:, g * N:(g + 1) * N].astype(BF16)
        cb = lax.dot_general(cg, bg, NT_DIMS, preferred_element_type=F32)
        sg = st_scr[:, g * GP:(g + 1) * GP]
        ys = jnp.dot(cg, sg.astype(BF16), preferred_element_type=F32)
        for hh in range(d.hpg):
            h = g * d.hpg + hh
            seg = cum[:, h:h + 1] - cum_t[h:h + 1, :]
            mat = (cb * jnp.exp(jnp.where(tril, seg, -jnp.inf))).astype(BF16)
            y_parts.append(jnp.dot(mat, xdt_b[:, h * P:(h + 1) * P], preferred_element_type=F32)
                           + ys[:, hh * P:(hh + 1) * P] * ecum_e[:, h * P:(h + 1) * P])
        st_scr[:, g * GP:(g + 1) * GP] = sg * elast_e[:, g * GP:(g + 1) * GP] + lax.dot_general(
            bg, xdd_b[:, g * GP:(g + 1) * GP], TN_DIMS, preferred_element_type=F32)
    y = jnp.concatenate(y_parts, axis=1) + dsk_ref[...] * xs
    z = proj_ref[:, d.off_z:d.off_z + SW]
    yz = y * (z * _sigmoid(z))
    ybuf_scr[:, :SW] = (yz * _rms_scale(yz, SW) * gss_ref[...]).astype(BF16)

    for hh in range(MH):
        qh = proj_ref[:, d.off_q + hh * DK:d.off_q + (hh + 1) * DK]
        kh = proj_ref[:, d.off_k + hh * DK:d.off_k + (hh + 1) * DK] * (DK ** -0.5)
        vh = proj_ref[:, d.off_v + hh * DV:d.off_v + (hh + 1) * DV]
        qb, kb, vb = qh.astype(BF16), kh.astype(BF16), vh.astype(BF16)
        bcol = bc[:, hh:hh + 1]
        brow = bc_t[hh:hh + 1, :]
        m_prev = m_scr[hh:hh + 1, 0:1]
        dm = jnp.where(tril, bcol - brow + li_t[hh:hh + 1, :], -jnp.inf)
        inter = bcol + m_prev
        m_t = jnp.maximum(inter, jnp.max(dm, axis=1, keepdims=True))
        w_inter = jnp.exp(inter - m_t)
        qkw = lax.dot_general(qb, kb, NT_DIMS, preferred_element_type=F32) * jnp.exp(dm - m_t)
        c_prev = c_scr[hh]
        n_prev = n_scr[hh:hh + 1, :]
        num = (jnp.dot(qkw.astype(BF16), vb, preferred_element_type=F32)
               + w_inter * jnp.dot(qb, c_prev.astype(BF16), preferred_element_type=F32))
        den = (jnp.sum(qkw, axis=1, keepdims=True)
               + w_inter * jnp.sum(qh * n_prev, axis=1, keepdims=True))
        hv = num / jnp.maximum(jnp.abs(den), jnp.exp(-m_t))
        b_last = bcol[L - 1:L, :]
        end_inter = b_last + m_prev
        m_new = jnp.maximum(end_inter, jnp.max(b_last - brow + li_t[hh:hh + 1, :], axis=1, keepdims=True))
        we = jnp.exp(b_last - bcol + li[:, hh:hh + 1] - m_new)
        carry = jnp.exp(end_inter - m_new)
        kw = kh * we
        c_scr[hh] = carry * c_prev + lax.dot_general(kw.astype(BF16), vb, TN_DIMS, preferred_element_type=F32)
        n_scr[hh:hh + 1, :] = carry * n_prev + jnp.sum(kw, axis=0, keepdims=True)
        m_scr[hh:hh + 1, :] = jnp.broadcast_to(m_new, (1, LANES))
        og = proj_ref[:, d.off_og + hh * DV:d.off_og + (hh + 1) * DV]
        hm = hv * _rms_scale(hv, DV) * gml_ref[:, hh * DV:(hh + 1) * DV] * _sigmoid(og)
        ybuf_scr[:, SW + hh * DV:SW + (hh + 1) * DV] = hm.astype(BF16)

    @pl.when(c > 0)
    def _():
        ycat_ref[...] = ybuf_scr[...]

    @pl.when(c == 0)
    def _():
        ymeta_ref[0] = ybuf_scr[pl.ds(pl.multiple_of(lo, d.n_meta), d.n_meta), :]

    @pl.when(c == pl.num_programs(1) - 1)
    def _():
        ssd_ref[0] = st_scr[...].T.reshape(H, P, N)
        conv_ref[0] = tail_scr[...]
        cst_ref[0] = c_scr[...]
        nst_ref[0] = n_scr[...]
        mst_ref[0] = m_scr[...]


def _prompt_scan(proj, lw, d: Dims):
    nchunk = d.seq // CHUNK
    meta_blk = (d.rows_main + d.rows_sample) // CHUNK
    width = d.ssd_width + d.ml_width

    def row_map(b, c):
        return (jnp.where(c == 0, meta_blk, b * nchunk + c - 1), 0)

    def out_map(b, c):
        return (b * nchunk + jnp.maximum(c - 1, 0), 0)

    const2 = lambda b, c: (0, 0)
    n_sub = _round_up(d.ml_heads, SUBLANES)
    return pl.pallas_call(
        functools.partial(_scan_kernel, d=d),
        grid=(d.batch, nchunk + 1),
        in_specs=[
            pl.BlockSpec((CHUNK, d.proj_cols), row_map),
            pl.BlockSpec((d.conv_w, d.conv_ch), const2),
            pl.BlockSpec((1, d.conv_ch), const2),
            pl.BlockSpec((SUBLANES, LANES), const2),
            pl.BlockSpec((1, d.ssd_width), const2),
            pl.BlockSpec((1, d.ssd_width), const2),
            pl.BlockSpec((1, d.ml_width), const2),
        ],
        out_specs=[
            pl.BlockSpec((CHUNK, width), out_map),
            pl.BlockSpec((1, d.n_meta, width), lambda b, c: (b, 0, 0)),
            pl.BlockSpec((1, d.ssd_heads, d.ssd_p, d.ssd_n), lambda b, c: (b, 0, 0, 0)),
            pl.BlockSpec((1, SUBLANES, d.conv_ch), lambda b, c: (b, 0, 0)),
            pl.BlockSpec((1, d.ml_heads, d.ml_dk, d.ml_dv), lambda b, c: (b, 0, 0, 0)),
            pl.BlockSpec((1, n_sub, d.ml_dk), lambda b, c: (b, 0, 0)),
            pl.BlockSpec((1, n_sub, LANES), lambda b, c: (b, 0, 0)),
        ],
        out_shape=[
            jax.ShapeDtypeStruct((d.rows_main, width), BF16),
            jax.ShapeDtypeStruct((d.batch, d.n_meta, width), BF16),
            jax.ShapeDtypeStruct((d.batch, d.ssd_heads, d.ssd_p, d.ssd_n), F32),
            jax.ShapeDtypeStruct((d.batch, SUBLANES, d.conv_ch), F32),
            jax.ShapeDtypeStruct((d.batch, d.ml_heads, d.ml_dk, d.ml_dv), F32),
            jax.ShapeDtypeStruct((d.batch, n_sub, d.ml_dk), F32),
            jax.ShapeDtypeStruct((d.batch, n_sub, LANES), F32),
        ],
        scratch_shapes=[
            pltpu.VMEM((d.ssd_n, d.ssd_width), F32),
            pltpu.VMEM((SUBLANES, d.conv_ch), F32),
            pltpu.VMEM((CHUNK, d.conv_ch), F32),
            pltpu.VMEM((d.ml_heads, d.ml_dk, d.ml_dv), F32),
            pltpu.VMEM((n_sub, d.ml_dk), F32),
            pltpu.VMEM((n_sub, LANES), F32),
            pltpu.VMEM((CHUNK, width), BF16),
        ],
        compiler_params=_cparams("parallel", "arbitrary"),
        name="prompt_scan",
    )(proj, lw["conv_w"], lw["conv_b"], lw["prm"], lw["d_skip_e"], lw["g_ssd"], lw["g_ml"])


SAMPLE_BT = SUBLANES


def _sample_ssd_kernel(proj_ref, hist_ref, sst_ref, cw_ref, cb_ref, prm_ref, dsk_ref, gss_ref, *rest,
                       d: Dims, aliased):
    y_ref, ossd_ref, oconv_ref = rest[1:] if aliased else rest
    H, P, N, G = d.ssd_heads, d.ssd_p, d.ssd_n, d.ssd_groups
    SW, CC, GP = d.ssd_width, d.conv_ch, d.hpg * d.ssd_p
    bt = SAMPLE_BT
    r0 = pl.multiple_of((pl.program_id(0) * bt) % CHUNK, bt)
    xp = proj_ref[pl.ds(r0, bt), d.off_xbc:d.off_xbc + CC]
    acc = cb_ref[...] + cw_ref[3:4, :] * xp
    for kk in range(d.conv_w - 1):
        acc = acc + cw_ref[kk:kk + 1, :] * hist_ref[kk]
    for kk in range(d.conv_w - 2):
        oconv_ref[kk] = hist_ref[kk + 1]
    oconv_ref[d.conv_w - 2] = xp
    xbc = acc * _sigmoid(acc)
    xs = xbc[:, :SW]
    bm = xbc[:, SW:SW + G * N]
    cm = xbc[:, SW + G * N:]

    lane = lax.broadcasted_iota(jnp.int32, (1, LANES), 1)
    is_h = lane < H
    dt = jnp.where(is_h, _softplus(proj_ref[pl.ds(r0, bt), d.off_dt:d.off_dt + LANES] + prm_ref[0:1, :]), 0.0)
    dec = jnp.where(is_h, jnp.exp(dt * -jnp.exp(prm_ref[1:2, :])), 0.0)
    e_ssd = _head_expand(H, P)
    xdt_t = _pad_rows(xs * _expand(dt, e_ssd), LANES).T
    dec_t = _pad_rows(_expand(dec, e_ssd), LANES).T
    lane_full = lax.broadcasted_iota(jnp.int32, (SW, LANES), 1)
    y_t = jnp.zeros((SW, LANES), F32)
    for i in range(bt):
        st = sst_ref[i].reshape(SW, N)
        b_full = jnp.concatenate(
            [jnp.broadcast_to(bm[i:i + 1, g * N:(g + 1) * N], (GP, N)) for g in range(G)], axis=0)
        c_full = jnp.concatenate(
            [jnp.broadcast_to(cm[i:i + 1, g * N:(g + 1) * N], (GP, N)) for g in range(G)], axis=0)
        new = st * dec_t[:, i:i + 1] + xdt_t[:, i:i + 1] * b_full
        ossd_ref[i] = new.reshape(H, P, N)
        y_t = jnp.where(lane_full == i, jnp.sum(new * c_full, axis=1, keepdims=True), y_t)
    y = y_t.T[:bt, :] + dsk_ref[...] * xs
    z = proj_ref[pl.ds(r0, bt), d.off_z:d.off_z + SW]
    yz = y * (z * _sigmoid(z))
    y_ref[...] = (yz * _rms_scale(yz, SW) * gss_ref[...]).astype(y_ref.dtype)


def _sample_ssd(proj, hist, sst_all, prev_ssd, l, lw, d: Dims):
    bt = SAMPLE_BT
    base_blk = d.rows_main // CHUNK
    const2 = lambda i: (0, 0)
    state_blk = pl.BlockSpec((None, bt, d.ssd_heads, d.ssd_p, d.ssd_n), lambda i: (l, i, 0, 0, 0))
    in_specs = [
        pl.BlockSpec((CHUNK, d.proj_cols), lambda i: (base_blk + (i * bt) // CHUNK, 0)),
        pl.BlockSpec((d.conv_w - 1, bt, d.conv_ch), lambda i: (0, i, 0)),
        state_blk,
        pl.BlockSpec((d.conv_w, d.conv_ch), const2),
        pl.BlockSpec((1, d.conv_ch), const2),
        pl.BlockSpec((SUBLANES, LANES), const2),
        pl.BlockSpec((1, d.ssd_width), const2),
        pl.BlockSpec((1, d.ssd_width), const2),
    ]
    args = [proj, hist, sst_all, lw["conv_w"], lw["conv_b"], lw["prm"], lw["d_skip_e"], lw["g_ssd"]]
    aliases = {}
    if prev_ssd is not None:
        in_specs.append(pl.BlockSpec(memory_space=pl.ANY))
        args.append(prev_ssd)
        aliases = {len(args) - 1: 1}
    return pl.pallas_call(
        functools.partial(_sample_ssd_kernel, d=d, aliased=prev_ssd is not None),
        grid=(d.dec_batch // bt,),
        in_specs=in_specs,
        out_specs=[
            pl.BlockSpec((bt, d.ssd_width), lambda i: (i, 0)),
            state_blk,
            pl.BlockSpec((d.conv_w - 1, bt, d.conv_ch), lambda i: (0, i, 0)),
        ],
        out_shape=[
            jax.ShapeDtypeStruct((d.dec_batch, d.ssd_width), F32),
            jax.ShapeDtypeStruct(sst_all.shape, F32),
            jax.ShapeDtypeStruct((d.conv_w - 1, d.dec_batch, d.conv_ch), F32),
        ],
        input_output_aliases=aliases,
        compiler_params=_cparams("parallel"),
        name="sample_ssd",
    )(*args)


def _sample_mlstm_kernel(proj_ref, cst_ref, nst_ref, mst_ref, prm_ref, gml_ref, *rest, d: Dims, aliased):
    y_ref, oc_ref, on_ref, om_ref = rest[1:] if aliased else rest
    MH, DK, DV = d.ml_heads, d.ml_dk, d.ml_dv
    QK, MW = d.ml_qk, d.ml_width
    bt = SAMPLE_BT
    r0 = pl.multiple_of((pl.program_id(0) * bt) % CHUNK, bt)
    rows = pl.ds(r0, bt)
    lane = lax.broadcasted_iota(jnp.int32, (1, LANES), 1)
    is_m = lane < MH
    li = proj_ref[rows, d.off_ig:d.off_ig + LANES] + prm_ref[2:3, :]
    lf = -_softplus(-(proj_ref[rows, d.off_fg:d.off_fg + LANES] + prm_ref[3:4, :]))
    inter = lf + mst_ref[...]
    m_t = jnp.maximum(inter, li)
    w_intra = jnp.where(is_m, jnp.exp(li - m_t), 0.0)
    w_inter = jnp.where(is_m, jnp.exp(inter - m_t), 0.0)
    om_ref[...] = m_t

    e_qk = _head_expand(MH, DK)
    e_v = _head_expand(MH, DV)
    q = proj_ref[rows, d.off_q:d.off_q + QK]
    k = proj_ref[rows, d.off_k:d.off_k + QK] * (DK ** -0.5)
    v = proj_ref[rows, d.off_v:d.off_v + MW]
    wi_e = _expand(w_intra, e_qk)
    wn_e = _expand(w_inter, e_qk)
    kw = k * wi_e
    n_new = wn_e * nst_ref[...] + kw
    on_ref[...] = n_new
    den = lax.dot_general(q * n_new, e_qk, NT_DIMS, precision=HIGHEST, preferred_element_type=F32)
    den = jnp.maximum(jnp.abs(den), jnp.exp(-m_t))
    kw_t = _pad_rows(kw, LANES).T
    q_t = _pad_rows(q, LANES).T
    wn_t = _pad_rows(wn_e, LANES).T
    num_rows = []
    for i in range(bt):
        parts = []
        for hh in range(MH):
            rs = slice(hh * DK, (hh + 1) * DK)
            c_new = cst_ref[i, hh] * wn_t[rs, i:i + 1] + kw_t[rs, i:i + 1] * v[i:i + 1, hh * DV:(hh + 1) * DV]
            oc_ref[i, hh] = c_new
            parts.append(jnp.sum(c_new * q_t[rs, i:i + 1], axis=0, keepdims=True))
        num_rows.append(jnp.concatenate(parts, axis=1))
    hv = jnp.concatenate(num_rows, axis=0) / _expand(den, e_v)
    ms = lax.dot_general(hv * hv, e_v, NT_DIMS, precision=HIGHEST, preferred_element_type=F32) * (1.0 / DV)
    og = proj_ref[rows, d.off_og:d.off_og + MW]
    y_ref[...] = (hv * _expand(lax.rsqrt(ms + RMS_EPS), e_v) * gml_ref[...] * _sigmoid(og)).astype(y_ref.dtype)


def _sample_mlstm(proj, cst_all, nst, mst, prev_c, l, lw, d: Dims):
    bt = SAMPLE_BT
    base_blk = d.rows_main // CHUNK
    const2 = lambda i: (0, 0)
    state_blk = pl.BlockSpec((None, bt, d.ml_heads, d.ml_dk, d.ml_dv), lambda i: (l, i, 0, 0, 0))
    in_specs = [
        pl.BlockSpec((CHUNK, d.proj_cols), lambda i: (base_blk + (i * bt) // CHUNK, 0)),
        state_blk,
        pl.BlockSpec((bt, d.ml_qk), lambda i: (i, 0)),
        pl.BlockSpec((bt, LANES), lambda i: (i, 0)),
        pl.BlockSpec((SUBLANES, LANES), const2),
        pl.BlockSpec((1, d.ml_width), const2),
    ]
    args = [proj, cst_all, nst, mst, lw["prm"], lw["g_ml"]]
    aliases = {}
    if prev_c is not None:
        in_specs.append(pl.BlockSpec(memory_space=pl.ANY))
        args.append(prev_c)
        aliases = {len(args) - 1: 1}
    return pl.pallas_call(
        functools.partial(_sample_mlstm_kernel, d=d, aliased=prev_c is not None),
        grid=(d.dec_batch // bt,),
        in_specs=in_specs,
        out_specs=[
            pl.BlockSpec((bt, d.ml_width), lambda i: (i, 0)),
            state_blk,
            pl.BlockSpec((bt, d.ml_qk), lambda i: (i, 0)),
            pl.BlockSpec((bt, LANES), lambda i: (i, 0)),
        ],
        out_shape=[
            jax.ShapeDtypeStruct((d.dec_batch, d.ml_width), F32),
            jax.ShapeDtypeStruct(cst_all.shape, F32),
            jax.ShapeDtypeStruct((d.dec_batch, d.ml_qk), F32),
            jax.ShapeDtypeStruct((d.dec_batch, LANES), F32),
        ],
        input_output_aliases=aliases,
        compiler_params=_cparams("parallel"),
        name="sample_mlstm",
    )(*args)


NO_ID = 1e9


def _topk_rows(s, ids, k):
    vals, sel = [], []
    for _ in range(k):
        m = jnp.max(s, axis=0, keepdims=True)
        idx = jnp.min(jnp.where(s == m, ids, NO_ID), axis=0, keepdims=True)
        vals.append(m)
        sel.append(idx)
        s = jnp.where(ids == idx, -jnp.inf, s)
    return jnp.concatenate(vals, axis=0), jnp.concatenate(sel, axis=0)


def _select_rows(sel, table, k):
    out = jnp.zeros(sel.shape, table.dtype)
    for a in range(k):
        out = jnp.where(sel == a, table[a:a + 1, :], out)
    return out


def _pair_candidates(v1, v2, k):
    tt = v1.shape[1]
    vals, ids = [], []

    def piece(values, flat_ids, lo, hi):
        row = lax.broadcasted_iota(jnp.int32, values.shape, 0)
        ok = (row >= lo) & (row < hi)
        vals.append(jnp.where(ok, values, -jnp.inf))
        ids.append(jnp.where(ok, flat_ids, NO_ID))

    def iota_f(rows):
        return lax.broadcasted_iota(jnp.int32, (rows, tt), 0).astype(F32)

    split = 0
    while (split + 1) * (split + 1) <= k:
        split += 1
    for a in range(split):
        nb = k // (a + 1)
        rows = _round_up(nb, SUBLANES)
        piece(v1[a:a + 1, :] + v2[:rows, :], a * k + iota_f(rows), 0, nb)
    for b in range(k):
        hi = k // (b + 1)
        if hi <= split:
            break
        rows = _round_up(hi, SUBLANES)
        piece(v1[:rows, :] + v2[b:b + 1, :], iota_f(rows) * k + b, split, hi)
    return jnp.concatenate(vals, axis=0), jnp.concatenate(ids, axis=0)


def _route(q_cols, tt, sk_ref, r_ref, c_ref, g_ref, heads, k):
    keys, half = sk_ref.shape[2], sk_ref.shape[3]
    key_ids = lax.broadcasted_iota(jnp.int32, (keys, tt), 0).astype(F32)
    r_parts, c_parts, g_parts = [], [], []
    for h in range(heads):
        tops = []
        for side in range(2):
            col = (2 * h + side) * half
            s = lax.dot_general(sk_ref[side, h], q_cols(col, half), NT_DIMS,
                                precision=HIGHEST, preferred_element_type=F32)
            tops.append(_topk_rows(s, key_ids, k))
        (v1, i1), (v2, i2) = tops
        sc, ci = _topk_rows(*_pair_candidates(v1, v2, k), k)
        a_sel = jnp.floor(ci * (1.0 / k))
        r_parts.append(_select_rows(a_sel, i1, k))
        c_parts.append(_select_rows(ci - a_sel * k, i2, k))
        e = jnp.exp(sc - sc[0:1, :])
        g_parts.append(e / jnp.sum(e, axis=0, keepdims=True))
    pad = LANES - heads * k
    if pad:
        r_parts.append(jnp.zeros((pad, tt), F32))
        c_parts.append(jnp.zeros((pad, tt), F32))
        g_parts.append(jnp.zeros((pad, tt), F32))
    r_ref[...] = jnp.concatenate(r_parts, axis=0).T.astype(jnp.int32)
    c_ref[...] = jnp.concatenate(c_parts, axis=0).T.astype(jnp.int32)
    g_ref[...] = jnp.concatenate(g_parts, axis=0).T


MIX_TM = LANES


def _mix_route_kernel(ymain_ref, ytail_ref, wo_ref, res_ref, g_ref, wq_ref, sk_ref,
                      h_ref, xn_ref, r_ref, c_ref, gate_ref, q_scr, *, n_main, heads, k):
    i = pl.program_id(0)

    @pl.when(i == 0)
    def _():
        q_scr[...] = jnp.zeros_like(q_scr)

    prev_slot = (i + 1) % 2
    _route(lambda col, width: q_scr[prev_slot, :, col:col + width], q_scr.shape[1],
           sk_ref, r_ref, c_ref, gate_ref, heads, k)
    a = jnp.where(i < n_main, ymain_ref[...], ytail_ref[...])
    h1 = res_ref[...] + jnp.dot(a, wo_ref[...], preferred_element_type=F32)
    h_ref[...] = h1
    xn = (h1 * _rms_scale(h1, h1.shape[-1]) * g_ref[...]).astype(BF16)
    xn_ref[...] = xn
    q_scr[i % 2] = jnp.dot(xn, wq_ref[...], preferred_element_type=F32)


def _mix_route(y_main, y_tail, w_out, res, gain, w_q, sub_keys, l, d: Dims):
    m, dm = res.shape
    k = y_main.shape[1]
    nq = w_q.shape[2]
    tm = MIX_TM
    n_main = y_main.shape[0] // tm
    n_tiles = n_main + y_tail.shape[0] // tm
    cur = lambda i: jnp.minimum(i, n_tiles - 1)
    prev = lambda i: jnp.maximum(i - 1, 0)
    rows = lambda width, which: pl.BlockSpec((tm, width), lambda i: (which(i), 0))
    resident = lambda shape: pl.BlockSpec(shape, lambda i: (l,) + (0,) * (len(shape) - 1),
                                          pipeline_mode=pl.Buffered(1))
    return pl.pallas_call(
        functools.partial(_mix_route_kernel, n_main=n_main, heads=d.peer_heads, k=PEER_TOPK),
        grid=(n_tiles + 1,),
        in_specs=[
            pl.BlockSpec((tm, k), lambda i: (jnp.minimum(i, n_main - 1), 0)),
            pl.BlockSpec((tm, k), lambda i: (jnp.clip(i - n_main, 0, n_tiles - n_main - 1), 0)),
            resident((None, k, dm)),
            rows(dm, cur),
            pl.BlockSpec((1, dm), lambda i: (0, 0)),
            resident((None, dm, nq)),
            resident((None,) + sub_keys.shape[1:]),
        ],
        out_specs=[rows(dm, cur), rows(dm, cur), rows(LANES, prev), rows(LANES, prev), rows(LANES, prev)],
        out_shape=[jax.ShapeDtypeStruct((m, dm), F32), jax.ShapeDtypeStruct((m, dm), BF16),
                   jax.ShapeDtypeStruct((m, LANES), jnp.int32), jax.ShapeDtypeStruct((m, LANES), jnp.int32),
                   jax.ShapeDtypeStruct((m, LANES), F32)],
        scratch_shapes=[pltpu.VMEM((2, tm, nq), F32)],
        compiler_params=_cparams("arbitrary"),
        name="mix_route",
    )(y_main, y_tail, w_out, res, gain.reshape(1, dm), w_q, sub_keys)


GATE_TT = 128
PAIR = 2 * SUBLANES
HIGH_HALF = 0xFFFF0000


def _gate_kernel(r_ref, c_ref, g_ref, w_ref):
    tt = r_ref.shape[0]
    sub = lax.broadcasted_iota(jnp.int32, (LANES, LANES), 0)
    zero = jnp.zeros((LANES, LANES), BF16)

    def gated_rows(row):
        on_r = jnp.broadcast_to(r_ref[row, :], (LANES, LANES)) == sub
        return jnp.where(on_r, jnp.broadcast_to(g_ref[row, :], (LANES, LANES)), 0.0).astype(BF16)

    def onehot_cols(row):
        return jnp.where(jnp.broadcast_to(c_ref[row, :], (LANES, LANES)) == sub, 1.0, 0.0).astype(BF16)

    def bf16_bits(x):
        return lax.bitcast_convert_type(x.astype(BF16).astype(F32), jnp.uint32)

    def per_pair(pg, carry):
        base = pl.multiple_of(pg * PAIR, PAIR)
        for tl in range(SUBLANES):
            ra = pl.ds(base + tl, 1)
            rb = pl.ds(base + SUBLANES + tl, 1)
            lhs = jnp.concatenate([gated_rows(ra), gated_rows(rb)], axis=1)
            rhs = jnp.concatenate([jnp.concatenate([onehot_cols(ra), zero], axis=1),
                                   jnp.concatenate([zero, onehot_cols(rb)], axis=1)], axis=0)
            w2 = lax.dot_general(lhs, rhs, NT_DIMS, preferred_element_type=F32)
            word = (lax.shift_right_logical(bf16_bits(w2[:, :LANES]), jnp.uint32(16))
                    | (bf16_bits(w2[:, LANES:]) & jnp.uint32(HIGH_HALF)))
            w_ref[pg, pl.ds(tl, LANES, stride=SUBLANES), :] = word
        return carry

    lax.fori_loop(0, tt // PAIR, per_pair, 0, unroll=4)


def _peer_gates(r_idx, c_idx, gate):
    m = r_idx.shape[0]
    tt = GATE_TT
    spec = pl.BlockSpec((tt, LANES), lambda i: (i, 0))
    return pl.pallas_call(
        _gate_kernel,
        grid=(m // tt,),
        in_specs=[spec, spec, spec],
        out_specs=pl.BlockSpec((tt // PAIR, LANES * SUBLANES, LANES), lambda i: (i, 0, 0)),
        out_shape=jax.ShapeDtypeStruct((m // PAIR, LANES * SUBLANES, LANES), jnp.uint32),
        compiler_params=_cparams("parallel"),
        name="peer_gates",
    )(r_idx, c_idx, gate)


def _gelu(x):
    return 0.5 * x * (1.0 + lax.erf(x * math.sqrt(0.5)))


def _peer_kernel(x_ref, u_ref, v_ref, w_ref, h_ref, o_ref):
    tt = x_ref.shape[0]
    ce = u_ref.shape[0]

    @pl.when(pl.program_id(1) == 0)
    def _():
        o_ref[...] = h_ref[...]

    a = lax.dot_general(x_ref[...], u_ref[...], NT_DIMS, preferred_element_type=F32)

    def gates_of(r):
        blk = w_ref[:, r * SUBLANES:(r + 1) * SUBLANES, :]
        even = lax.bitcast_convert_type(lax.shift_left(blk, jnp.uint32(16)), F32)
        odd = lax.bitcast_convert_type(blk & jnp.uint32(HIGH_HALF), F32)
        return jnp.stack([even, odd], axis=1).reshape(tt, LANES)

    w = jnp.concatenate([gates_of(r) for r in range(ce // LANES)], axis=1)
    o_ref[...] += jnp.dot((_gelu(a) * w).astype(BF16), v_ref[...], preferred_element_type=F32)


def _peer_experts(x, u, v, w, h, l):
    m, k = x.shape
    e = u.shape[1]
    tt = _pick_tile(m, 768, CHUNK)
    ce = _pick_tile(e, 1024, LANES)
    rows_spec = pl.BlockSpec((tt, k), lambda i, j: (i, 0))
    tab_spec = pl.BlockSpec((None, ce, k), lambda i, j: (l, j, 0))
    return pl.pallas_call(
        _peer_kernel,
        grid=(m // tt, e // ce),
        in_specs=[
            rows_spec, tab_spec, tab_spec,
            pl.BlockSpec((tt // PAIR, ce // LANES * SUBLANES, LANES), lambda i, j: (i, j, 0)),
            pl.BlockSpec((tt, k), lambda i, j: (i, 0), pipeline_mode=pl.Buffered(1)),
        ],
        out_specs=rows_spec,
        out_shape=jax.ShapeDtypeStruct((m, k), F32),
        compiler_params=_cparams("parallel", "arbitrary"),
        name="peer_experts",
    )(x, u, v, w, h)


def _lane_row(x):
    return jnp.pad(x.astype(F32), (0, LANES - x.shape[0])).reshape(1, LANES)


def _pack_w_in(w_in, d: Dims):
    sizes = (d.ssd_width, d.conv_ch, d.ssd_heads, d.ml_qk, d.ml_qk, d.ml_width, d.ml_heads, d.ml_heads, d.ml_width)
    offs = [0]
    for s in sizes:
        offs.append(offs[-1] + s)
    w_t = jnp.swapaxes(w_in, 1, 2)
    piece = lambda i: w_t[:, offs[i]:offs[i + 1], :]
    pad = lambda w: jnp.pad(w, ((0, 0), (0, LANES - w.shape[1]), (0, 0)))
    w_z, w_xbc, w_dt, w_q, w_k, w_v, w_ig, w_fg, w_og = [piece(i) for i in range(9)]
    return jnp.concatenate(
        [w_xbc, w_q, w_k, w_z, w_og, w_v, pad(w_dt), pad(w_ig), pad(w_fg)], axis=1).astype(BF16)


def _pack_layer(l, d: Dims, conv_w, conv_b, dt_bias, a_log, d_skip, g_ssd_out, i_bias, f_bias, g_mlstm_out):
    prm = jnp.concatenate(
        [_lane_row(dt_bias[l]), _lane_row(a_log[l]), _lane_row(i_bias[l]), _lane_row(f_bias[l]),
         jnp.zeros((SUBLANES - 4, LANES), F32)], axis=0)
    return dict(
        conv_w=conv_w[l], conv_b=conv_b[l].reshape(1, -1), prm=prm,
        d_skip_e=jnp.repeat(d_skip[l], d.ssd_p).reshape(1, -1),
        g_ssd=g_ssd_out[l].reshape(1, -1), g_ml=g_mlstm_out[l].reshape(1, -1),
    )


def _derive_dims(x_prompt, x_sample, state_ssd, state_conv, state_mlstm_c, meta_tokens, peer_sub_keys, peer_u):
    batch, seq, d_model = x_prompt.shape
    _, dec_batch, heads, p, n = state_ssd.shape
    conv_ch = state_conv.shape[-1]
    groups = (conv_ch - heads * p) // (2 * n)
    _, _, mh, dk, dv = state_mlstm_c.shape
    _, ph, keys, half = peer_sub_keys.shape[1:]
    d = Dims(d_model=d_model, batch=batch, seq=seq, dec_batch=dec_batch, n_meta=meta_tokens.shape[0],
             ssd_heads=heads, ssd_p=p, ssd_n=n, ssd_groups=groups, conv_w=state_conv.shape[2] + 1,
             ml_heads=mh, ml_dk=dk, ml_dv=dv, peer_heads=ph, peer_keys=keys, peer_half=half,
             n_experts=peer_u.shape[1])
    assert x_sample.shape[1] == 1 and seq % CHUNK == 0
    assert d.batch * d.n_meta <= CHUNK and d.n_meta % 16 == 0
    assert d.dec_batch % SAMPLE_BT == 0
    assert d.ssd_n == LANES and d.peer_keys == LANES and d.peer_half == LANES
    assert d.ssd_heads <= LANES and d.ml_heads <= SUBLANES and d.conv_w == 4
    assert d.peer_heads * PEER_TOPK <= LANES and d.n_experts == d.peer_keys ** 2
    assert PEER_TOPK & (PEER_TOPK - 1) == 0
    assert d.rows_main % MIX_TM == 0 and (d.rows_sample + d.rows_meta) % MIX_TM == 0
    assert all(w % LANES == 0 for w in (d.ssd_width, d.ml_qk, d.ml_width, d.hpg * d.ssd_p, d.ml_dk))
    return d


def kernel(x_prompt, x_sample, state_ssd, state_conv, state_mlstm_c, state_mlstm_n, state_mlstm_m, meta_tokens, g_mix, w_in, conv_w, conv_b, dt_bias, a_log, d_skip, g_ssd_out, mlstm_i_bias, mlstm_f_bias, g_mlstm_out, w_out, g_ffn, peer_w_q, peer_sub_keys, peer_u, peer_v, g_final):
    d = _derive_dims(x_prompt, x_sample, state_ssd, state_conv, state_mlstm_c, meta_tokens, peer_sub_keys, peer_u)
    depth = w_in.shape[0]
    dm = d.d_model
    meta = jnp.broadcast_to(meta_tokens[None].astype(F32), (d.batch, d.n_meta, dm)).reshape(-1, dm)
    h = jnp.concatenate([
        x_prompt.reshape(-1, dm),
        _pad_rows(x_sample.reshape(-1, dm), d.rows_sample),
        _pad_rows(meta, d.rows_meta),
    ], axis=0)

    w_in_b = _pack_w_in(w_in, d)
    w_out_b, w_q_b = w_out.astype(BF16), peer_w_q.astype(BF16)
    u_b, v_b = peer_u.astype(BF16), peer_v.astype(BF16)

    new_p = ([], [], [], [], [])
    new_s = ([], [], [])
    s_ssd = s_c = None
    for l in range(depth):
        lw = _pack_layer(l, d, conv_w, conv_b, dt_bias, a_log, d_skip, g_ssd_out, mlstm_i_bias,
                         mlstm_f_bias, g_mlstm_out)
        proj, _ = _norm_matmul(h, g_mix[l], w_in_b, l)

        y_main, y_meta, p_ssd, p_conv, p_c, p_n, p_m = _prompt_scan(proj, lw, d)
        s_yssd, s_ssd, s_conv = _sample_ssd(
            proj, jnp.transpose(state_conv[l], (1, 0, 2)), state_ssd, s_ssd, l, lw, d)
        m_pad = jnp.pad(state_mlstm_m[l], ((0, 0), (0, LANES - d.ml_heads)))
        s_yml, s_c, s_n, s_m = _sample_mlstm(
            proj, state_mlstm_c, state_mlstm_n[l].reshape(d.dec_batch, d.ml_qk), m_pad, s_c, l, lw, d)

        y_tail = jnp.concatenate([
            _pad_rows(jnp.concatenate([s_yssd, s_yml], axis=1).astype(BF16), d.rows_sample),
            _pad_rows(y_meta.reshape(d.batch * d.n_meta, -1), d.rows_meta),
        ], axis=0)
        h, xn, r_idx, c_idx, gate = _mix_route(y_main, y_tail, w_out_b, h, g_ffn[l], w_q_b, peer_sub_keys, l, d)
        h = _peer_experts(xn, u_b, v_b, _peer_gates(r_idx, c_idx, gate), h, l)

        for lst, s in zip(new_p, (p_ssd, p_conv[:, SUBLANES - (d.conv_w - 1):, :], p_c,
                                  p_n[:, :d.ml_heads, :], p_m[:, :d.ml_heads, 0])):
            lst.append(s)
        for lst, s in zip(new_s, (jnp.transpose(s_conv, (1, 0, 2)),
                                  s_n.reshape(d.dec_batch, d.ml_heads, d.ml_dk), s_m[:, :d.ml_heads])):
            lst.append(s)

    y_prompt = _rmsnorm_rows(h, g_final, 0, d.rows_main).reshape(d.batch, d.seq, dm)
    y_sample = _rmsnorm_rows(h, g_final, d.rows_main, d.rows_sample)[:d.dec_batch].reshape(d.dec_batch, 1, dm)
    s_conv_all, s_n_all, s_m_all = (jnp.stack(s) for s in new_s)
    return ((y_prompt, y_sample) + tuple(jnp.stack(s) for s in new_p)
            + (s_ssd, s_conv_all, s_c, s_n_all, s_m_all))
```

```python
import functools
import math
from typing import NamedTuple

import jax
import jax.numpy as jnp
from jax import lax
from jax.experimental import pallas as pl
from jax.experimental.pallas import tpu as pltpu

F32 = jnp.float32
BF16 = jnp.bfloat16
HIGHEST = lax.Precision.HIGHEST

LANES = 128
SUBLANES = 8
V7X_VMEM_LIMIT_BYTES = 56 * 1024 * 1024

CHUNK = 128
PEER_TOPK = 16
RMS_EPS = 1e-6
NT_DIMS = (((1,), (1,)), ((), ()))
TN_DIMS = (((0,), (0,)), ((), ()))


class Dims(NamedTuple):
    d_model: int
    batch: int
    seq: int
    dec_batch: int
    n_meta: int
    ssd_heads: int
    ssd_p: int
    ssd_n: int
    ssd_groups: int
    conv_w: int
    ml_heads: int
    ml_dk: int
    ml_dv: int
    peer_heads: int
    peer_keys: int
    peer_half: int
    n_experts: int

    @property
    def ssd_width(self):
        return self.ssd_heads * self.ssd_p

    @property
    def conv_ch(self):
        return self.ssd_width + 2 * self.ssd_groups * self.ssd_n

    @property
    def ml_width(self):
        return self.ml_heads * self.ml_dv

    @property
    def ml_qk(self):
        return self.ml_heads * self.ml_dk

    @property
    def hpg(self):
        return self.ssd_heads // self.ssd_groups

    @property
    def rows_main(self):
        return self.batch * self.seq

    @property
    def rows_sample(self):
        return _round_up(self.dec_batch, CHUNK)

    @property
    def rows_meta(self):
        return CHUNK

    @property
    def rows(self):
        return self.rows_main + self.rows_sample + self.rows_meta

    @property
    def off_xbc(self):
        return 0

    @property
    def off_q(self):
        return self.conv_ch

    @property
    def off_k(self):
        return self.off_q + self.ml_qk

    @property
    def off_z(self):
        return self.off_k + self.ml_qk

    @property
    def off_og(self):
        return self.off_z + self.ssd_width

    @property
    def off_v(self):
        return self.off_og + self.ml_width

    @property
    def off_dt(self):
        return self.off_v + self.ml_width

    @property
    def off_ig(self):
        return self.off_dt + LANES

    @property
    def off_fg(self):
        return self.off_ig + LANES

    @property
    def proj_cols(self):
        return self.off_fg + LANES


def _round_up(x, m):
    return (x + m - 1) // m * m


def _pick_tile(n, target, quantum):
    best = None
    t = quantum
    while t <= min(n, target):
        if n % t == 0:
            best = t
        t += quantum
    return best if best is not None else n


def _cparams(*sem):
    return pltpu.CompilerParams(dimension_semantics=sem, vmem_limit_bytes=V7X_VMEM_LIMIT_BYTES)


def _sigmoid(x):
    return 1.0 / (1.0 + jnp.exp(-x))


def _softplus(x):
    return jnp.maximum(x, 0.0) + jnp.log1p(jnp.exp(-jnp.abs(x)))


def _rms_scale(x, width):
    return lax.rsqrt(jnp.sum(x * x, axis=-1, keepdims=True) * (1.0 / width) + RMS_EPS)


def _head_expand(n_heads, per_head, total_rows=LANES):
    row = lax.broadcasted_iota(jnp.int32, (total_rows, n_heads * per_head), 0)
    lane = lax.broadcasted_iota(jnp.int32, (total_rows, n_heads * per_head), 1)
    return jnp.where((lane >= row * per_head) & (lane < (row + 1) * per_head), 1.0, 0.0).astype(F32)


def _expand_many(xs, onehot):
    x = jnp.concatenate(xs, axis=0) if len(xs) > 1 else xs[0]
    p1 = x.astype(BF16)
    r1 = x - p1.astype(F32)
    p2 = r1.astype(BF16)
    p3 = (r1 - p2.astype(F32)).astype(BF16)
    e = onehot.astype(BF16)
    out = jnp.dot(jnp.concatenate([p1, p2, p3], axis=1), jnp.concatenate([e, e, e], axis=0),
                  preferred_element_type=F32)
    outs, start = [], 0
    for xi in xs:
        outs.append(out[start:start + xi.shape[0], :])
        start += xi.shape[0]
    return outs


def _expand(x, onehot):
    return _expand_many([x], onehot)[0]


def _pad_rows(x, rows):
    if x.shape[0] == rows:
        return x
    return jnp.concatenate([x, jnp.zeros((rows - x.shape[0], x.shape[1]), x.dtype)], axis=0)


def _norm_matmul_kernel(x_ref, g_ref, w_ref, o_ref, hn_ref):
    @pl.when(pl.program_id(1) == 0)
    def _():
        x = x_ref[...]
        hn_ref[...] = (x * _rms_scale(x, x.shape[-1]) * g_ref[...]).astype(hn_ref.dtype)

    o_ref[...] = lax.dot_general(hn_ref[...], w_ref[...], NT_DIMS, preferred_element_type=F32)


def _norm_matmul(x, gain, w_t, l):
    m, k = x.shape
    n = w_t.shape[1]
    tm = _pick_tile(m, 768, CHUNK)
    tn = _pick_tile(n, 1792, LANES)
    return pl.pallas_call(
        _norm_matmul_kernel,
        grid=(m // tm, n // tn),
        in_specs=[
            pl.BlockSpec((tm, k), lambda i, j: (i, 0)),
            pl.BlockSpec((1, k), lambda i, j: (0, 0)),
            pl.BlockSpec((None, tn, k), lambda i, j: (l, j, 0)),
        ],
        out_specs=[
            pl.BlockSpec((tm, tn), lambda i, j: (i, j)),
            pl.BlockSpec((tm, k), lambda i, j: (i, 0)),
        ],
        out_shape=[jax.ShapeDtypeStruct((m, n), F32), jax.ShapeDtypeStruct((m, k), BF16)],
        compiler_params=_cparams("parallel", "arbitrary"),
        name="norm_matmul",
    )(x, gain.reshape(1, k), w_t)


def _rmsnorm_kernel(x_ref, g_ref, o_ref):
    x = x_ref[...]
    o_ref[...] = x * _rms_scale(x, x.shape[-1]) * g_ref[...]


def _rmsnorm_rows(x, gain, row_start, n_rows):
    k = x.shape[1]
    tm = _pick_tile(n_rows, 512, CHUNK)
    assert row_start % tm == 0
    first = row_start // tm
    return pl.pallas_call(
        _rmsnorm_kernel,
        grid=(n_rows // tm,),
        in_specs=[pl.BlockSpec((tm, k), lambda i: (first + i, 0)), pl.BlockSpec((1, k), lambda i: (0, 0))],
        out_specs=pl.BlockSpec((tm, k), lambda i: (i, 0)),
        out_shape=jax.ShapeDtypeStruct((n_rows, k), F32),
        compiler_params=_cparams("parallel"),
        name="final_rmsnorm",
    )(x, gain.reshape(1, k))


def _scan_kernel(proj_ref, cw_ref, cb_ref, prm_ref, dsk_ref, gss_ref, gml_ref,
                 ycat_ref, ymeta_ref, ssd_ref, conv_ref, cst_ref, nst_ref, mst_ref,
                 st_scr, tail_scr, cbuf_scr, c_scr, n_scr, m_scr, ybuf_scr, *, d: Dims):
    b = pl.program_id(0)
    c = pl.program_id(1)
    L = CHUNK
    H, P, N, G = d.ssd_heads, d.ssd_p, d.ssd_n, d.ssd_groups
    SW, CC, GP = d.ssd_width, d.conv_ch, d.hpg * d.ssd_p
    MH, DK, DV = d.ml_heads, d.ml_dk, d.ml_dv

    @pl.when(c == 0)
    def _init():
        st_scr[...] = jnp.zeros_like(st_scr)
        tail_scr[...] = jnp.zeros_like(tail_scr)
        c_scr[...] = jnp.zeros_like(c_scr)
        n_scr[...] = jnp.zeros_like(n_scr)
        m_scr[...] = jnp.zeros_like(m_scr)

    rowi = lax.broadcasted_iota(jnp.int32, (L, 1), 0)
    lo = b * d.n_meta
    valid = (c > 0) | ((rowi >= lo) & (rowi < lo + d.n_meta))

    xp = jnp.where(valid, proj_ref[:, d.off_xbc:d.off_xbc + CC], 0.0)
    cbuf_scr[...] = xp
    tail = tail_scr[...]
    sub8 = lax.broadcasted_iota(jnp.int32, (SUBLANES, 1), 0)
    acc = cb_ref[...] + cw_ref[d.conv_w - 1:d.conv_w, :] * xp
    for kk in range(d.conv_w - 1):
        shift = d.conv_w - 1 - kk
        rolled = pltpu.roll(xp, shift, axis=0)
        head = jnp.where(sub8 < shift, pltpu.roll(tail, shift, axis=0), rolled[:SUBLANES, :])
        acc = acc + cw_ref[kk:kk + 1, :] * jnp.concatenate([head, rolled[SUBLANES:, :]], axis=0)
    xbc = acc * _sigmoid(acc)
    ts = pl.multiple_of(jnp.where(c == 0, lo + d.n_meta - SUBLANES, L - SUBLANES), SUBLANES)
    tail_scr[...] = cbuf_scr[pl.ds(ts, SUBLANES), :]
    xs = xbc[:, :SW]
    bm = xbc[:, SW:SW + G * N]
    cm = xbc[:, SW + G * N:]

    lane = lax.broadcasted_iota(jnp.int32, (1, LANES), 1)
    is_h = lane < H
    is_m = lane < MH
    dt = jnp.where(valid & is_h, _softplus(proj_ref[:, d.off_dt:d.off_dt + LANES] + prm_ref[0:1, :]), 0.0)
    a_neg = jnp.where(is_h, -jnp.exp(prm_ref[1:2, :]), 0.0)
    li = jnp.where(valid, proj_ref[:, d.off_ig:d.off_ig + LANES] + prm_ref[2:3, :], -jnp.inf)
    lf = jnp.where(valid & is_m, -_softplus(-(proj_ref[:, d.off_fg:d.off_fg + LANES] + prm_ref[3:4, :])), 0.0)
    r2 = lax.broadcasted_iota(jnp.int32, (L, L), 0)
    c2 = lax.broadcasted_iota(jnp.int32, (L, L), 1)
    tril = c2 <= r2
    tri = jnp.where(tril, 1.0, 0.0).astype(F32)
    cums = jnp.dot(tri, jnp.concatenate([dt * a_neg, lf], axis=1), precision=HIGHEST,
                   preferred_element_type=F32)
    cum = cums[:, :LANES]
    bc = cums[:, LANES:]
    cum_t = cum.T
    bc_t = bc.T
    li_t = li.T
    cum_last = cum[L - 1:L, :]

    dt_e, ecum_e, dend_e, elast_e = _expand_many(
        [dt, jnp.where(is_h, jnp.exp(cum), 0.0), jnp.where(is_h, jnp.exp(cum_last - cum), 0.0),
         jnp.broadcast_to(jnp.where(is_h, jnp.exp(cum_last), 0.0), (SUBLANES, LANES))],
        _head_expand(H, P))
    elast_e = elast_e[0:1, :]

    xdt = xs * dt_e
    xdt_b = xdt.astype(BF16)
    xdd_b = (xdt * dend_e).astype(BF16)
    y_parts = []
    for g in range(G):
        cg = cm[:, g * N:(g + 1) * N].astype(BF16)
        bg = bm[:, g * N:(g + 1) * N].astype(BF16)
        cb = lax.dot_general(cg, bg, NT_DIMS, preferred_element_type=F32)
        sg = st_scr[:, g * GP:(g + 1) * GP]
        ys = jnp.dot(cg, sg.astype(BF16), preferred_element_type=F32)
        for hh in range(d.hpg):
            h = g * d.hpg + hh
            seg = cum[:, h:h + 1] - cum_t[h:h + 1, :]
            mat = (cb * jnp.exp(jnp.where(tril, seg, -jnp.inf))).astype(BF16)
            y_parts.append(jnp.dot(mat, xdt_b[:, h * P:(h + 1) * P], preferred_element_type=F32)
                           + ys[:, hh * P:(hh + 1) * P] * ecum_e[:, h * P:(h + 1) * P])
        st_scr[:, g * GP:(g + 1) * GP] = sg * elast_e[:, g * GP:(g + 1) * GP] + lax.dot_general(
            bg, xdd_b[:, g * GP:(g + 1) * GP], TN_DIMS, preferred_element_type=F32)
    y = jnp.concatenate(y_parts, axis=1) + dsk_ref[...] * xs
    z = proj_ref[:, d.off_z:d.off_z + SW]
    yz = y * (z * _sigmoid(z))
    ybuf_scr[:, :SW] = (yz * _rms_scale(yz, SW) * gss_ref[...]).astype(BF16)

    for hh in range(MH):
        qh = proj_ref[:, d.off_q + hh * DK:d.off_q + (hh + 1) * DK]
        kh = proj_ref[:, d.off_k + hh * DK:d.off_k + (hh + 1) * DK] * (DK ** -0.5)
        vh = proj_ref[:, d.off_v + hh * DV:d.off_v + (hh + 1) * DV]
        qb, kb, vb = qh.astype(BF16), kh.astype(BF16), vh.astype(BF16)
        bcol = bc[:, hh:hh + 1]
        brow = bc_t[hh:hh + 1, :]
        m_prev = m_scr[hh:hh + 1, 0:1]
        dm = jnp.where(tril, bcol - brow + li_t[hh:hh + 1, :], -jnp.inf)
        inter = bcol + m_prev
        m_t = jnp.maximum(inter, jnp.max(dm, axis=1, keepdims=True))
        w_inter = jnp.exp(inter - m_t)
        qkw = lax.dot_general(qb, kb, NT_DIMS, preferred_element_type=F32) * jnp.exp(dm - m_t)
        c_prev = c_scr[hh]
        n_prev = n_scr[hh:hh + 1, :]
        num = (jnp.dot(qkw.astype(BF16), vb, preferred_element_type=F32)
               + w_inter * jnp.dot(qb, c_prev.astype(BF16), preferred_element_type=F32))
        den = (jnp.sum(qkw, axis=1, keepdims=True)
               + w_inter * jnp.sum(qh * n_prev, axis=1, keepdims=True))
        hv = num / jnp.maximum(jnp.abs(den), jnp.exp(-m_t))
        b_last = bcol[L - 1:L, :]
        end_inter = b_last + m_prev
        m_new = jnp.maximum(end_inter, jnp.max(b_last - brow + li_t[hh:hh + 1, :], axis=1, keepdims=True))
        we = jnp.exp(b_last - bcol + li[:, hh:hh + 1] - m_new)
        carry = jnp.exp(end_inter - m_new)
        kw = kh * we
        c_scr[hh] = carry * c_prev + lax.dot_general(kw.astype(BF16), vb, TN_DIMS, preferred_element_type=F32)
        n_scr[hh:hh + 1, :] = carry * n_prev + jnp.sum(kw, axis=0, keepdims=True)
        m_scr[hh:hh + 1, :] = jnp.broadcast_to(m_new, (1, LANES))
        og = proj_ref[:, d.off_og + hh * DV:d.off_og + (hh + 1) * DV]
        hm = hv * _rms_scale(hv, DV) * gml_ref[:, hh * DV:(hh + 1) * DV] * _sigmoid(og)
        ybuf_scr[:, SW + hh * DV:SW + (hh + 1) * DV] = hm.astype(BF16)

    @pl.when(c > 0)
    def _():
        ycat_ref[...] = ybuf_scr[...]

    @pl.when(c == 0)
    def _():
        ymeta_ref[0] = ybuf_scr[pl.ds(pl.multiple_of(lo, d.n_meta), d.n_meta), :]

    @pl.when(c == pl.num_programs(1) - 1)
    def _():
        ssd_ref[0] = st_scr[...].T.reshape(H, P, N)
        conv_ref[0] = tail_scr[...]
        cst_ref[0] = c_scr[...]
        nst_ref[0] = n_scr[...]
        mst_ref[0] = m_scr[...]


def _prompt_scan(proj, lw, d: Dims):
    nchunk = d.seq // CHUNK
    meta_blk = (d.rows_main + d.rows_sample) // CHUNK
    width = d.ssd_width + d.ml_width

    def row_map(b, c):
        return (jnp.where(c == 0, meta_blk, b * nchunk + c - 1), 0)

    def out_map(b, c):
        return (b * nchunk + jnp.maximum(c - 1, 0), 0)

    const2 = lambda b, c: (0, 0)
    n_sub = _round_up(d.ml_heads, SUBLANES)
    return pl.pallas_call(
        functools.partial(_scan_kernel, d=d),
        grid=(d.batch, nchunk + 1),
        in_specs=[
            pl.BlockSpec((CHUNK, d.proj_cols), row_map),
            pl.BlockSpec((d.conv_w, d.conv_ch), const2),
            pl.BlockSpec((1, d.conv_ch), const2),
            pl.BlockSpec((SUBLANES, LANES), const2),
            pl.BlockSpec((1, d.ssd_width), const2),
            pl.BlockSpec((1, d.ssd_width), const2),
            pl.BlockSpec((1, d.ml_width), const2),
        ],
        out_specs=[
            pl.BlockSpec((CHUNK, width), out_map),
            pl.BlockSpec((1, d.n_meta, width), lambda b, c: (b, 0, 0)),
            pl.BlockSpec((1, d.ssd_heads, d.ssd_p, d.ssd_n), lambda b, c: (b, 0, 0, 0)),
            pl.BlockSpec((1, SUBLANES, d.conv_ch), lambda b, c: (b, 0, 0)),
            pl.BlockSpec((1, d.ml_heads, d.ml_dk, d.ml_dv), lambda b, c: (b, 0, 0, 0)),
            pl.BlockSpec((1, n_sub, d.ml_dk), lambda b, c: (b, 0, 0)),
            pl.BlockSpec((1, n_sub, LANES), lambda b, c: (b, 0, 0)),
        ],
        out_shape=[
            jax.ShapeDtypeStruct((d.rows_main, width), BF16),
            jax.ShapeDtypeStruct((d.batch, d.n_meta, width), BF16),
            jax.ShapeDtypeStruct((d.batch, d.ssd_heads, d.ssd_p, d.ssd_n), F32),
            jax.ShapeDtypeStruct((d.batch, SUBLANES, d.conv_ch), F32),
            jax.ShapeDtypeStruct((d.batch, d.ml_heads, d.ml_dk, d.ml_dv), F32),
            jax.ShapeDtypeStruct((d.batch, n_sub, d.ml_dk), F32),
            jax.ShapeDtypeStruct((d.batch, n_sub, LANES), F32),
        ],
        scratch_shapes=[
            pltpu.VMEM((d.ssd_n, d.ssd_width), F32),
            pltpu.VMEM((SUBLANES, d.conv_ch), F32),
            pltpu.VMEM((CHUNK, d.conv_ch), F32),
            pltpu.VMEM((d.ml_heads, d.ml_dk, d.ml_dv), F32),
            pltpu.VMEM((n_sub, d.ml_dk), F32),
            pltpu.VMEM((n_sub, LANES), F32),
            pltpu.VMEM((CHUNK, width), BF16),
        ],
        compiler_params=_cparams("parallel", "arbitrary"),
        name="prompt_scan",
    )(proj, lw["conv_w"], lw["conv_b"], lw["prm"], lw["d_skip_e"], lw["g_ssd"], lw["g_ml"])


SAMPLE_BT = SUBLANES


def _sample_ssd_kernel(proj_ref, hist_ref, sst_ref, cw_ref, cb_ref, prm_ref, dsk_ref, gss_ref, *rest,
                       d: Dims, aliased):
    y_ref, ossd_ref, oconv_ref = rest[1:] if aliased else rest
    H, P, N, G = d.ssd_heads, d.ssd_p, d.ssd_n, d.ssd_groups
    SW, CC, GP = d.ssd_width, d.conv_ch, d.hpg * d.ssd_p
    bt = SAMPLE_BT
    r0 = pl.multiple_of((pl.program_id(0) * bt) % CHUNK, bt)
    xp = proj_ref[pl.ds(r0, bt), d.off_xbc:d.off_xbc + CC]
    acc = cb_ref[...] + cw_ref[3:4, :] * xp
    for kk in range(d.conv_w - 1):
        acc = acc + cw_ref[kk:kk + 1, :] * hist_ref[kk]
    for kk in range(d.conv_w - 2):
        oconv_ref[kk] = hist_ref[kk + 1]
    oconv_ref[d.conv_w - 2] = xp
    xbc = acc * _sigmoid(acc)
    xs = xbc[:, :SW]
    bm = xbc[:, SW:SW + G * N]
    cm = xbc[:, SW + G * N:]

    lane = lax.broadcasted_iota(jnp.int32, (1, LANES), 1)
    is_h = lane < H
    dt = jnp.where(is_h, _softplus(proj_ref[pl.ds(r0, bt), d.off_dt:d.off_dt + LANES] + prm_ref[0:1, :]), 0.0)
    dec = jnp.where(is_h, jnp.exp(dt * -jnp.exp(prm_ref[1:2, :])), 0.0)
    e_ssd = _head_expand(H, P)
    xdt_t = _pad_rows(xs * _expand(dt, e_ssd), LANES).T
    dec_t = _pad_rows(_expand(dec, e_ssd), LANES).T
    lane_full = lax.broadcasted_iota(jnp.int32, (SW, LANES), 1)
    y_t = jnp.zeros((SW, LANES), F32)
    for i in range(bt):
        st = sst_ref[i].reshape(SW, N)
        b_full = jnp.concatenate(
            [jnp.broadcast_to(bm[i:i + 1, g * N:(g + 1) * N], (GP, N)) for g in range(G)], axis=0)
        c_full = jnp.concatenate(
            [jnp.broadcast_to(cm[i:i + 1, g * N:(g + 1) * N], (GP, N)) for g in range(G)], axis=0)
        new = st * dec_t[:, i:i + 1] + xdt_t[:, i:i + 1] * b_full
        ossd_ref[i] = new.reshape(H, P, N)
        y_t = jnp.where(lane_full == i, jnp.sum(new * c_full, axis=1, keepdims=True), y_t)
    y = y_t.T[:bt, :] + dsk_ref[...] * xs
    z = proj_ref[pl.ds(r0, bt), d.off_z:d.off_z + SW]
    yz = y * (z * _sigmoid(z))
    y_ref[...] = (yz * _rms_scale(yz, SW) * gss_ref[...]).astype(y_ref.dtype)


def _sample_ssd(proj, hist, sst_all, prev_ssd, l, lw, d: Dims):
    bt = SAMPLE_BT
    base_blk = d.rows_main // CHUNK
    const2 = lambda i: (0, 0)
    state_blk = pl.BlockSpec((None, bt, d.ssd_heads, d.ssd_p, d.ssd_n), lambda i: (l, i, 0, 0, 0))
    in_specs = [
        pl.BlockSpec((CHUNK, d.proj_cols), lambda i: (base_blk + (i * bt) // CHUNK, 0)),
        pl.BlockSpec((d.conv_w - 1, bt, d.conv_ch), lambda i: (0, i, 0)),
        state_blk,
        pl.BlockSpec((d.conv_w, d.conv_ch), const2),
        pl.BlockSpec((1, d.conv_ch), const2),
        pl.BlockSpec((SUBLANES, LANES), const2),
        pl.BlockSpec((1, d.ssd_width), const2),
        pl.BlockSpec((1, d.ssd_width), const2),
    ]
    args = [proj, hist, sst_all, lw["conv_w"], lw["conv_b"], lw["prm"], lw["d_skip_e"], lw["g_ssd"]]
    aliases = {}
    if prev_ssd is not None:
        in_specs.append(pl.BlockSpec(memory_space=pl.ANY))
        args.append(prev_ssd)
        aliases = {len(args) - 1: 1}
    return pl.pallas_call(
        functools.partial(_sample_ssd_kernel, d=d, aliased=prev_ssd is not None),
        grid=(d.dec_batch // bt,),
        in_specs=in_specs,
        out_specs=[
            pl.BlockSpec((bt, d.ssd_width), lambda i: (i, 0)),
            state_blk,
            pl.BlockSpec((d.conv_w - 1, bt, d.conv_ch), lambda i: (0, i, 0)),
        ],
        out_shape=[
            jax.ShapeDtypeStruct((d.dec_batch, d.ssd_width), F32),
            jax.ShapeDtypeStruct(sst_all.shape, F32),
            jax.ShapeDtypeStruct((d.conv_w - 1, d.dec_batch, d.conv_ch), F32),
        ],
        input_output_aliases=aliases,
        compiler_params=_cparams("parallel"),
        name="sample_ssd",
    )(*args)


def _sample_mlstm_kernel(proj_ref, cst_ref, nst_ref, mst_ref, prm_ref, gml_ref, *rest, d: Dims, aliased):
    y_ref, oc_ref, on_ref, om_ref = rest[1:] if aliased else rest
    MH, DK, DV = d.ml_heads, d.ml_dk, d.ml_dv
    QK, MW = d.ml_qk, d.ml_width
    bt = SAMPLE_BT
    r0 = pl.multiple_of((pl.program_id(0) * bt) % CHUNK, bt)
    rows = pl.ds(r0, bt)
    lane = lax.broadcasted_iota(jnp.int32, (1, LANES), 1)
    is_m = lane < MH
    li = proj_ref[rows, d.off_ig:d.off_ig + LANES] + prm_ref[2:3, :]
    lf = -_softplus(-(proj_ref[rows, d.off_fg:d.off_fg + LANES] + prm_ref[3:4, :]))
    inter = lf + mst_ref[...]
    m_t = jnp.maximum(inter, li)
    w_intra = jnp.where(is_m, jnp.exp(li - m_t), 0.0)
    w_inter = jnp.where(is_m, jnp.exp(inter - m_t), 0.0)
    om_ref[...] = m_t

    e_qk = _head_expand(MH, DK)
    e_v = _head_expand(MH, DV)
    q = proj_ref[rows, d.off_q:d.off_q + QK]
    k = proj_ref[rows, d.off_k:d.off_k + QK] * (DK ** -0.5)
    v = proj_ref[rows, d.off_v:d.off_v + MW]
    wi_e = _expand(w_intra, e_qk)
    wn_e = _expand(w_inter, e_qk)
    kw = k * wi_e
    n_new = wn_e * nst_ref[...] + kw
    on_ref[...] = n_new
    den = lax.dot_general(q * n_new, e_qk, NT_DIMS, precision=HIGHEST, preferred_element_type=F32)
    den = jnp.maximum(jnp.abs(den), jnp.exp(-m_t))
    kw_t = _pad_rows(kw, LANES).T
    q_t = _pad_rows(q, LANES).T
    wn_t = _pad_rows(wn_e, LANES).T
    num_rows = []
    for i in range(bt):
        parts = []
        for hh in range(MH):
            rs = slice(hh * DK, (hh + 1) * DK)
            c_new = cst_ref[i, hh] * wn_t[rs, i:i + 1] + kw_t[rs, i:i + 1] * v[i:i + 1, hh * DV:(hh + 1) * DV]
            oc_ref[i, hh] = c_new
            parts.append(jnp.sum(c_new * q_t[rs, i:i + 1], axis=0, keepdims=True))
        num_rows.append(jnp.concatenate(parts, axis=1))
    hv = jnp.concatenate(num_rows, axis=0) / _expand(den, e_v)
    ms = lax.dot_general(hv * hv, e_v, NT_DIMS, precision=HIGHEST, preferred_element_type=F32) * (1.0 / DV)
    og = proj_ref[rows, d.off_og:d.off_og + MW]
    y_ref[...] = (hv * _expand(lax.rsqrt(ms + RMS_EPS), e_v) * gml_ref[...] * _sigmoid(og)).astype(y_ref.dtype)


def _sample_mlstm(proj, cst_all, nst, mst, prev_c, l, lw, d: Dims):
    bt = SAMPLE_BT
    base_blk = d.rows_main // CHUNK
    const2 = lambda i: (0, 0)
    state_blk = pl.BlockSpec((None, bt, d.ml_heads, d.ml_dk, d.ml_dv), lambda i: (l, i, 0, 0, 0))
    in_specs = [
        pl.BlockSpec((CHUNK, d.proj_cols), lambda i: (base_blk + (i * bt) // CHUNK, 0)),
        state_blk,
        pl.BlockSpec((bt, d.ml_qk), lambda i: (i, 0)),
        pl.BlockSpec((bt, LANES), lambda i: (i, 0)),
        pl.BlockSpec((SUBLANES, LANES), const2),
        pl.BlockSpec((1, d.ml_width), const2),
    ]
    args = [proj, cst_all, nst, mst, lw["prm"], lw["g_ml"]]
    aliases = {}
    if prev_c is not None:
        in_specs.append(pl.BlockSpec(memory_space=pl.ANY))
        args.append(prev_c)
        aliases = {len(args) - 1: 1}
    return pl.pallas_call(
        functools.partial(_sample_mlstm_kernel, d=d, aliased=prev_c is not None),
        grid=(d.dec_batch // bt,),
        in_specs=in_specs,
        out_specs=[
            pl.BlockSpec((bt, d.ml_width), lambda i: (i, 0)),
            state_blk,
            pl.BlockSpec((bt, d.ml_qk), lambda i: (i, 0)),
            pl.BlockSpec((bt, LANES), lambda i: (i, 0)),
        ],
        out_shape=[
            jax.ShapeDtypeStruct((d.dec_batch, d.ml_width), F32),
            jax.ShapeDtypeStruct(cst_all.shape, F32),
            jax.ShapeDtypeStruct((d.dec_batch, d.ml_qk), F32),
            jax.ShapeDtypeStruct((d.dec_batch, LANES), F32),
        ],
        input_output_aliases=aliases,
        compiler_params=_cparams("parallel"),
        name="sample_mlstm",
    )(*args)


NO_ID = 1e9


def _topk_rows(s, ids, k):
    vals, sel = [], []
    for _ in range(k):
        m = jnp.max(s, axis=0, keepdims=True)
        idx = jnp.min(jnp.where(s == m, ids, NO_ID), axis=0, keepdims=True)
        vals.append(m)
        sel.append(idx)
        s = jnp.where(ids == idx, -jnp.inf, s)
    return jnp.concatenate(vals, axis=0), jnp.concatenate(sel, axis=0)


def _select_rows(sel, table, k):
    out = jnp.zeros(sel.shape, table.dtype)
    for a in range(k):
        out = jnp.where(sel == a, table[a:a + 1, :], out)
    return out


def _pair_candidates(v1, v2, k):
    tt = v1.shape[1]
    vals, ids = [], []

    def piece(values, flat_ids, lo, hi):
        row = lax.broadcasted_iota(jnp.int32, values.shape, 0)
        ok = (row >= lo) & (row < hi)
        vals.append(jnp.where(ok, values, -jnp.inf))
        ids.append(jnp.where(ok, flat_ids, NO_ID))

    def iota_f(rows):
        return lax.broadcasted_iota(jnp.int32, (rows, tt), 0).astype(F32)

    split = 0
    while (split + 1) * (split + 1) <= k:
        split += 1
    for a in range(split):
        nb = k // (a + 1)
        rows = _round_up(nb, SUBLANES)
        piece(v1[a:a + 1, :] + v2[:rows, :], a * k + iota_f(rows), 0, nb)
    for b in range(k):
        hi = k // (b + 1)
        if hi <= split:
            break
        rows = _round_up(hi, SUBLANES)
        piece(v1[:rows, :] + v2[b:b + 1, :], iota_f(rows) * k + b, split, hi)
    return jnp.concatenate(vals, axis=0), jnp.concatenate(ids, axis=0)


def _route(q_cols, tt, sk_ref, r_ref, c_ref, g_ref, heads, k):
    keys, half = sk_ref.shape[2], sk_ref.shape[3]
    key_ids = lax.broadcasted_iota(jnp.int32, (keys, tt), 0).astype(F32)
    r_parts, c_parts, g_parts = [], [], []
    for h in range(heads):
        tops = []
        for side in range(2):
            col = (2 * h + side) * half
            s = lax.dot_general(sk_ref[side, h], q_cols(col, half), NT_DIMS,
                                precision=HIGHEST, preferred_element_type=F32)
            tops.append(_topk_rows(s, key_ids, k))
        (v1, i1), (v2, i2) = tops
        sc, ci = _topk_rows(*_pair_candidates(v1, v2, k), k)
        a_sel = jnp.floor(ci * (1.0 / k))
        r_parts.append(_select_rows(a_sel, i1, k))
        c_parts.append(_select_rows(ci - a_sel * k, i2, k))
        e = jnp.exp(sc - sc[0:1, :])
        g_parts.append(e / jnp.sum(e, axis=0, keepdims=True))
    pad = LANES - heads * k
    if pad:
        r_parts.append(jnp.zeros((pad, tt), F32))
        c_parts.append(jnp.zeros((pad, tt), F32))
        g_parts.append(jnp.zeros((pad, tt), F32))
    r_ref[...] = jnp.concatenate(r_parts, axis=0).T.astype(jnp.int32)
    c_ref[...] = jnp.concatenate(c_parts, axis=0).T.astype(jnp.int32)
    g_ref[...] = jnp.concatenate(g_parts, axis=0).T


MIX_TM = LANES


def _mix_route_kernel(ymain_ref, ytail_ref, wo_ref, res_ref, g_ref, wq_ref, sk_ref,
                      h_ref, xn_ref, r_ref, c_ref, gate_ref, q_scr, *, n_main, heads, k):
    i = pl.program_id(0)

    @pl.when(i == 0)
    def _():
        q_scr[...] = jnp.zeros_like(q_scr)

    prev_slot = (i + 1) % 2
    _route(lambda col, width: q_scr[prev_slot, :, col:col + width], q_scr.shape[1],
           sk_ref, r_ref, c_ref, gate_ref, heads, k)
    a = jnp.where(i < n_main, ymain_ref[...], ytail_ref[...])
    h1 = res_ref[...] + jnp.dot(a, wo_ref[...], preferred_element_type=F32)
    h_ref[...] = h1
    xn = (h1 * _rms_scale(h1, h1.shape[-1]) * g_ref[...]).astype(BF16)
    xn_ref[...] = xn
    q_scr[i % 2] = jnp.dot(xn, wq_ref[...], preferred_element_type=F32)


def _mix_route(y_main, y_tail, w_out, res, gain, w_q, sub_keys, l, d: Dims):
    m, dm = res.shape
    k = y_main.shape[1]
    nq = w_q.shape[2]
    tm = MIX_TM
    n_main = y_main.shape[0] // tm
    n_tiles = n_main + y_tail.shape[0] // tm
    cur = lambda i: jnp.minimum(i, n_tiles - 1)
    prev = lambda i: jnp.maximum(i - 1, 0)
    rows = lambda width, which: pl.BlockSpec((tm, width), lambda i: (which(i), 0))
    resident = lambda shape: pl.BlockSpec(shape, lambda i: (l,) + (0,) * (len(shape) - 1),
                                          pipeline_mode=pl.Buffered(1))
    return pl.pallas_call(
        functools.partial(_mix_route_kernel, n_main=n_main, heads=d.peer_heads, k=PEER_TOPK),
        grid=(n_tiles + 1,),
        in_specs=[
            pl.BlockSpec((tm, k), lambda i: (jnp.minimum(i, n_main - 1), 0)),
            pl.BlockSpec((tm, k), lambda i: (jnp.clip(i - n_main, 0, n_tiles - n_main - 1), 0)),
            resident((None, k, dm)),
            rows(dm, cur),
            pl.BlockSpec((1, dm), lambda i: (0, 0)),
            resident((None, dm, nq)),
            resident((None,) + sub_keys.shape[1:]),
        ],
        out_specs=[rows(dm, cur), rows(dm, cur), rows(LANES, prev), rows(LANES, prev), rows(LANES, prev)],
        out_shape=[jax.ShapeDtypeStruct((m, dm), F32), jax.ShapeDtypeStruct((m, dm), BF16),
                   jax.ShapeDtypeStruct((m, LANES), jnp.int32), jax.ShapeDtypeStruct((m, LANES), jnp.int32),
                   jax.ShapeDtypeStruct((m, LANES), F32)],
        scratch_shapes=[pltpu.VMEM((2, tm, nq), F32)],
        compiler_params=_cparams("arbitrary"),
        name="mix_route",
    )(y_main, y_tail, w_out, res, gain.reshape(1, dm), w_q, sub_keys)


GATE_TT = 128
PAIR = 2 * SUBLANES


def _gate_tiles(r_ref, c_ref, g_ref, w_ref):
    tt = r_ref.shape[0]
    sub = lax.broadcasted_iota(jnp.int32, (LANES, LANES), 0)
    zero = jnp.zeros((LANES, LANES), BF16)

    def gated_rows(row):
        on_r = jnp.broadcast_to(r_ref[row, :], (LANES, LANES)) == sub
        return jnp.where(on_r, jnp.broadcast_to(g_ref[row, :], (LANES, LANES)), 0.0).astype(BF16)

    def onehot_cols(row):
        return jnp.where(jnp.broadcast_to(c_ref[row, :], (LANES, LANES)) == sub, 1.0, 0.0).astype(BF16)

    def per_pair(pg, carry):
        base = pl.multiple_of(pg * PAIR, PAIR)
        for tl in range(SUBLANES):
            ra = pl.ds(base + tl, 1)
            rb = pl.ds(base + SUBLANES + tl, 1)
            lhs = jnp.concatenate([gated_rows(ra), gated_rows(rb)], axis=1)
            rhs = jnp.concatenate([jnp.concatenate([onehot_cols(ra), zero], axis=1),
                                   jnp.concatenate([zero, onehot_cols(rb)], axis=1)], axis=0)
            w2 = lax.dot_general(lhs, rhs, NT_DIMS, preferred_element_type=F32)
            w_ref[2 * pg, pl.ds(tl, LANES, stride=SUBLANES), :] = w2[:, :LANES]
            w_ref[2 * pg + 1, pl.ds(tl, LANES, stride=SUBLANES), :] = w2[:, LANES:]
        return carry

    lax.fori_loop(0, tt // PAIR, per_pair, 0, unroll=4)


def _peer_gates(r_idx, c_idx, gate):
    m = r_idx.shape[0]
    tt = GATE_TT
    spec = pl.BlockSpec((tt, LANES), lambda i: (i, 0))
    return pl.pallas_call(
        _gate_tiles,
        grid=(m // tt,),
        in_specs=[spec, spec, spec],
        out_specs=pl.BlockSpec((tt // SUBLANES, LANES * SUBLANES, LANES), lambda i: (i, 0, 0)),
        out_shape=jax.ShapeDtypeStruct((m // SUBLANES, LANES * SUBLANES, LANES), F32),
        compiler_params=_cparams("parallel"),
        name="peer_gates",
    )(r_idx, c_idx, gate)


def _gelu(x):
    return 0.5 * x * (1.0 + lax.erf(x * math.sqrt(0.5)))


def _peer_kernel(x_ref, u_ref, v_ref, w_ref, h_ref, o_ref):
    tt = x_ref.shape[0]
    ce = u_ref.shape[0]

    @pl.when(pl.program_id(1) == 0)
    def _():
        o_ref[...] = h_ref[...]

    a = lax.dot_general(x_ref[...], u_ref[...], NT_DIMS, preferred_element_type=F32)

    w = jnp.concatenate(
        [w_ref[:, r * SUBLANES:(r + 1) * SUBLANES, :].reshape(tt, LANES) for r in range(ce // LANES)], axis=1)
    o_ref[...] += jnp.dot((_gelu(a) * w).astype(BF16), v_ref[...], preferred_element_type=F32)


def _peer_experts(x, u, v, w, h, l):
    m, k = x.shape
    e = u.shape[1]
    tt = _pick_tile(m, 768, CHUNK)
    ce = _pick_tile(e, 1024, LANES)
    rows_spec = pl.BlockSpec((tt, k), lambda i, j: (i, 0))
    tab_spec = pl.BlockSpec((None, ce, k), lambda i, j: (l, j, 0))
    return pl.pallas_call(
        _peer_kernel,
        grid=(m // tt, e // ce),
        in_specs=[
            rows_spec, tab_spec, tab_spec,
            pl.BlockSpec((tt // SUBLANES, ce // LANES * SUBLANES, LANES), lambda i, j: (i, j, 0)),
            pl.BlockSpec((tt, k), lambda i, j: (i, 0), pipeline_mode=pl.Buffered(1)),
        ],
        out_specs=rows_spec,
        out_shape=jax.ShapeDtypeStruct((m, k), F32),
        compiler_params=_cparams("parallel", "arbitrary"),
        name="peer_experts",
    )(x, u, v, w, h)


def _lane_row(x):
    return jnp.pad(x.astype(F32), (0, LANES - x.shape[0])).reshape(1, LANES)


def _pack_w_in(w_in, d: Dims):
    sizes = (d.ssd_width, d.conv_ch, d.ssd_heads, d.ml_qk, d.ml_qk, d.ml_width, d.ml_heads, d.ml_heads, d.ml_width)
    offs = [0]
    for s in sizes:
        offs.append(offs[-1] + s)
    w_t = jnp.swapaxes(w_in, 1, 2)
    piece = lambda i: w_t[:, offs[i]:offs[i + 1], :]
    pad = lambda w: jnp.pad(w, ((0, 0), (0, LANES - w.shape[1]), (0, 0)))
    w_z, w_xbc, w_dt, w_q, w_k, w_v, w_ig, w_fg, w_og = [piece(i) for i in range(9)]
    return jnp.concatenate(
        [w_xbc, w_q, w_k, w_z, w_og, w_v, pad(w_dt), pad(w_ig), pad(w_fg)], axis=1).astype(BF16)


def _pack_layer(l, d: Dims, conv_w, conv_b, dt_bias, a_log, d_skip, g_ssd_out, i_bias, f_bias, g_mlstm_out):
    prm = jnp.concatenate(
        [_lane_row(dt_bias[l]), _lane_row(a_log[l]), _lane_row(i_bias[l]), _lane_row(f_bias[l]),
         jnp.zeros((SUBLANES - 4, LANES), F32)], axis=0)
    return dict(
        conv_w=conv_w[l], conv_b=conv_b[l].reshape(1, -1), prm=prm,
        d_skip_e=jnp.repeat(d_skip[l], d.ssd_p).reshape(1, -1),
        g_ssd=g_ssd_out[l].reshape(1, -1), g_ml=g_mlstm_out[l].reshape(1, -1),
    )


def _derive_dims(x_prompt, x_sample, state_ssd, state_conv, state_mlstm_c, meta_tokens, peer_sub_keys, peer_u):
    batch, seq, d_model = x_prompt.shape
    _, dec_batch, heads, p, n = state_ssd.shape
    conv_ch = state_conv.shape[-1]
    groups = (conv_ch - heads * p) // (2 * n)
    _, _, mh, dk, dv = state_mlstm_c.shape
    _, ph, keys, half = peer_sub_keys.shape[1:]
    d = Dims(d_model=d_model, batch=batch, seq=seq, dec_batch=dec_batch, n_meta=meta_tokens.shape[0],
             ssd_heads=heads, ssd_p=p, ssd_n=n, ssd_groups=groups, conv_w=state_conv.shape[2] + 1,
             ml_heads=mh, ml_dk=dk, ml_dv=dv, peer_heads=ph, peer_keys=keys, peer_half=half,
             n_experts=peer_u.shape[1])
    assert x_sample.shape[1] == 1 and seq % CHUNK == 0
    assert d.batch * d.n_meta <= CHUNK and d.n_meta % 16 == 0
    assert d.dec_batch % SAMPLE_BT == 0
    assert d.ssd_n == LANES and d.peer_keys == LANES and d.peer_half == LANES
    assert d.ssd_heads <= LANES and d.ml_heads <= SUBLANES and d.conv_w == 4
    assert d.peer_heads * PEER_TOPK <= LANES and d.n_experts == d.peer_keys ** 2
    assert PEER_TOPK & (PEER_TOPK - 1) == 0
    assert d.rows_main % MIX_TM == 0 and (d.rows_sample + d.rows_meta) % MIX_TM == 0
    assert all(w % LANES == 0 for w in (d.ssd_width, d.ml_qk, d.ml_width, d.hpg * d.ssd_p, d.ml_dk))
    return d


def kernel(x_prompt, x_sample, state_ssd, state_conv, state_mlstm_c, state_mlstm_n, state_mlstm_m, meta_tokens, g_mix, w_in, conv_w, conv_b, dt_bias, a_log, d_skip, g_ssd_out, mlstm_i_bias, mlstm_f_bias, g_mlstm_out, w_out, g_ffn, peer_w_q, peer_sub_keys, peer_u, peer_v, g_final):
    d = _derive_dims(x_prompt, x_sample, state_ssd, state_conv, state_mlstm_c, meta_tokens, peer_sub_keys, peer_u)
    depth = w_in.shape[0]
    dm = d.d_model
    meta = jnp.broadcast_to(meta_tokens[None].astype(F32), (d.batch, d.n_meta, dm)).reshape(-1, dm)
    h = jnp.concatenate([
        x_prompt.reshape(-1, dm),
        _pad_rows(x_sample.reshape(-1, dm), d.rows_sample),
        _pad_rows(meta, d.rows_meta),
    ], axis=0)

    w_in_b = _pack_w_in(w_in, d)
    w_out_b, w_q_b = w_out.astype(BF16), peer_w_q.astype(BF16)
    u_b, v_b = peer_u.astype(BF16), peer_v.astype(BF16)

    new_p = ([], [], [], [], [])
    new_s = ([], [], [])
    s_ssd = s_c = None
    for l in range(depth):
        lw = _pack_layer(l, d, conv_w, conv_b, dt_bias, a_log, d_skip, g_ssd_out, mlstm_i_bias,
                         mlstm_f_bias, g_mlstm_out)
        proj, _ = _norm_matmul(h, g_mix[l], w_in_b, l)

        y_main, y_meta, p_ssd, p_conv, p_c, p_n, p_m = _prompt_scan(proj, lw, d)
        s_yssd, s_ssd, s_conv = _sample_ssd(
            proj, jnp.transpose(state_conv[l], (1, 0, 2)), state_ssd, s_ssd, l, lw, d)
        m_pad = jnp.pad(state_mlstm_m[l], ((0, 0), (0, LANES - d.ml_heads)))
        s_yml, s_c, s_n, s_m = _sample_mlstm(
            proj, state_mlstm_c, state_mlstm_n[l].reshape(d.dec_batch, d.ml_qk), m_pad, s_c, l, lw, d)

        y_tail = jnp.concatenate([
            _pad_rows(jnp.concatenate([s_yssd, s_yml], axis=1).astype(BF16), d.rows_sample),
            _pad_rows(y_meta.reshape(d.batch * d.n_meta, -1), d.rows_meta),
        ], axis=0)
        h, xn, r_idx, c_idx, gate = _mix_route(y_main, y_tail, w_out_b, h, g_ffn[l], w_q_b, peer_sub_keys, l, d)
        h = _peer_experts(xn, u_b, v_b, _peer_gates(r_idx, c_idx, gate), h, l)

        for lst, s in zip(new_p, (p_ssd, p_conv[:, SUBLANES - (d.conv_w - 1):, :], p_c,
                                  p_n[:, :d.ml_heads, :], p_m[:, :d.ml_heads, 0])):
            lst.append(s)
        for lst, s in zip(new_s, (jnp.transpose(s_conv, (1, 0, 2)),
                                  s_n.reshape(d.dec_batch, d.ml_heads, d.ml_dk), s_m[:, :d.ml_heads])):
            lst.append(s)

    y_prompt = _rmsnorm_rows(h, g_final, 0, d.rows_main).reshape(d.batch, d.seq, dm)
    y_sample = _rmsnorm_rows(h, g_final, d.rows_main, d.rows_sample)[:d.dec_batch].reshape(d.dec_batch, 1, dm)
    s_conv_all, s_n_all, s_m_all = (jnp.stack(s) for s in new_s)
    return ((y_prompt, y_sample) + tuple(jnp.stack(s) for s in new_p)
            + (s_ssd, s_conv_all, s_c, s_n_all, s_m_all))
```
